```python
import jax, jax.numpy as jnp
from jax import lax
import numpy as np

D_MODEL = 1024
BATCH = 2
SEQ = 8192
DEPTH = 4
DEC_BATCH = 128
DEC_SEQ = 4
PAST_LEN = 8192
PAGE_SIZE = 128

HEAD_DIM = 64
N_HEADS = D_MODEL // HEAD_DIM
N_KV_HEADS = N_HEADS // 4
GROUP = N_HEADS // N_KV_HEADS
BRANCH = N_HEADS * HEAD_DIM
KV_WIDTH = N_KV_HEADS * HEAD_DIM
IDX_HEADS = N_HEADS // 2
IDX_DIM = 64
TOPK_MAX = 256
WINDOW = 128
BLOCK = 128
ROPE_THETA = 10000.0
RMS_EPS = 1e-6
N_MIXERS = 2
N_A_LAYERS = (DEPTH + N_MIXERS - 1) // N_MIXERS
N_B_LAYERS = DEPTH // N_MIXERS
A_COLS = (BRANCH, KV_WIDTH, KV_WIDTH, BRANCH, IDX_HEADS * IDX_DIM, IDX_DIM, IDX_HEADS)
B_COLS = (BRANCH, KV_WIDTH, KV_WIDTH, BRANCH)

kernel_name = "hybrid_dsa_swa_sink_gated_decode_step"


def _rmsnorm(x, g):
    xf = x.astype(jnp.float32)
    y = xf * lax.rsqrt(jnp.mean(xf * xf, axis=-1, keepdims=True) + RMS_EPS)
    return (y * g.astype(jnp.float32)).astype(x.dtype)


def _split(u, sizes):
    return jnp.split(u, np.cumsum(sizes)[:-1].tolist(), axis=-1)


def _rope(x, pos):
    d = x.shape[-1]
    half = d // 2
    inv = ROPE_THETA ** (-jnp.arange(half, dtype=jnp.float32) * 2.0 / d)
    ang = pos.astype(jnp.float32)[:, None] * inv[None, :]
    cos = jnp.cos(ang)[:, None, :]
    sin = jnp.sin(ang)[:, None, :]
    xf = x.astype(jnp.float32)
    x1, x2 = xf[..., :half], xf[..., half:]
    return jnp.concatenate([x1 * cos - x2 * sin, x1 * sin + x2 * cos], axis=-1).astype(x.dtype)


def _project_a(h, g, w_in, pos):
    n, t, _ = h.shape
    u = _rmsnorm(h, g) @ w_in
    q, k, v, z, iq, ik, iw = _split(u, A_COLS)
    q = _rope(q.reshape(n, t, N_HEADS, HEAD_DIM), pos)
    k = _rope(k.reshape(n, t, N_KV_HEADS, HEAD_DIM), pos)
    v = v.reshape(n, t, N_KV_HEADS, HEAD_DIM)
    iq = _rope(iq.reshape(n, t, IDX_HEADS, IDX_DIM), pos)
    ik = _rope(ik.reshape(n, t, 1, IDX_DIM), pos)[:, :, 0]
    return q, k, v, z, iq, ik, iw


def _project_b(h, g, w_in, pos):
    n, t, _ = h.shape
    u = _rmsnorm(h, g) @ w_in
    q, k, v, z = _split(u, B_COLS)
    q = _rope(q.reshape(n, t, N_HEADS, HEAD_DIM), pos)
    k = _rope(k.reshape(n, t, N_KV_HEADS, HEAD_DIM), pos)
    v = v.reshape(n, t, N_KV_HEADS, HEAD_DIM)
    return q, k, v, z


def _gated_out(o, z, w_out):
    n, t = o.shape[:2]
    return (o.reshape(n, t, BRANCH) * jax.nn.silu(z)) @ w_out


def _select_keys(iq, iw, ik, q_pos, k_pos, n_keep):
    s = jnp.einsum('nthd,nld->nthl', iq.astype(jnp.float32), ik.astype(jnp.float32)) * (IDX_DIM ** -0.5)
    score = jnp.einsum('nth,nthl->ntl', iw.astype(jnp.float32) * (IDX_HEADS ** -0.5), jax.nn.relu(s))
    admissible = k_pos[None, :] <= q_pos[:, None]
    score = jnp.where(admissible[None], score, -jnp.inf)
    _, sel = lax.top_k(score, n_keep)
    valid = sel <= q_pos[None, :, None]
    return sel, valid


def _gathered_attention(q, ks, vs, valid):
    n, t = q.shape[:2]
    qg = q.reshape(n, t, N_KV_HEADS, GROUP, HEAD_DIM)
    s = jnp.einsum('ntkgd,ntskd->ntkgs', qg, ks).astype(jnp.float32) * (HEAD_DIM ** -0.5)
    s = jnp.where(valid[:, :, None, None, :], s, -jnp.inf)
    p = jax.nn.softmax(s, axis=-1).astype(vs.dtype)
    o = jnp.einsum('ntkgs,ntskd->ntkgd', p, vs)
    return o.reshape(n, t, N_HEADS, HEAD_DIM)


def _dsa_prompt(q, k, v, iq, iw, ik):
    b, s_len = q.shape[:2]
    nb = s_len // BLOCK
    n_keep = min(TOPK_MAX, s_len // 4)
    k_pos = jnp.arange(s_len, dtype=jnp.int32)
    q_pos_blocks = k_pos.reshape(nb, BLOCK)
    bidx = jnp.arange(b)[:, None, None]

    def blockify(a):
        return jnp.moveaxis(a.reshape(b, nb, BLOCK, *a.shape[2:]), 1, 0)

    def one(args):
        qb, iqb, iwb, qp = args
        sel, valid = _select_keys(iqb, iwb, ik, qp, k_pos, n_keep)
        return _gathered_attention(qb, k[bidx, sel], v[bidx, sel], valid)

    out = lax.map(one, (blockify(q), blockify(iq), blockify(iw), q_pos_blocks))
    return jnp.moveaxis(out, 0, 1).reshape(b, s_len, N_HEADS, HEAD_DIM)


def _dsa_sample(q, k_new, v_new, iq, iw, ik_new, cache_k, cache_v, cache_ik, la, page_table, q_pos):
    db, t = q.shape[:2]
    past = page_table.shape[1] * PAGE_SIZE
    ik_past = cache_ik[la, page_table].reshape(db, past, IDX_DIM)
    ik_all = jnp.concatenate([ik_past, ik_new], axis=1)
    l_len = past + t
    k_pos = jnp.arange(l_len, dtype=jnp.int32)
    sel, valid = _select_keys(iq, iw, ik_all, q_pos, k_pos, min(TOPK_MAX, l_len // 4))
    bidx = jnp.arange(db)[:, None, None]
    is_past = (sel < past)[..., None, None]
    sp = jnp.minimum(sel, past - 1)
    pg = page_table[bidx, sp // PAGE_SIZE]
    off = sp % PAGE_SIZE
    sn = jnp.clip(sel - past, 0, t - 1)
    ks = jnp.where(is_past, cache_k[la, pg, off], k_new[bidx, sn])
    vs = jnp.where(is_past, cache_v[la, pg, off], v_new[bidx, sn])
    return _gathered_attention(q, ks, vs, valid)


def _sink_attention(q, k, v, mask, sinks_l):
    s = jnp.einsum('...qkgd,...skd->...kgqs', q, k).astype(jnp.float32) * (HEAD_DIM ** -0.5)
    s = jnp.where(mask, s, -jnp.inf)
    sink = jnp.broadcast_to(sinks_l.astype(jnp.float32).reshape(N_KV_HEADS, GROUP, 1, 1), s.shape[:-1] + (1,))
    p = jax.nn.softmax(jnp.concatenate([s, sink], axis=-1), axis=-1)[..., :-1]
    return jnp.einsum('...kgqs,...skd->...qkgd', p.astype(v.dtype), v)


def _swa_prompt(q, k, v, sinks_l):
    b, s_len = q.shape[:2]
    nb = s_len // BLOCK
    qb = q.reshape(b, nb, BLOCK, N_KV_HEADS, GROUP, HEAD_DIM)
    kb = k.reshape(b, nb, BLOCK, N_KV_HEADS, HEAD_DIM)
    vb = v.reshape(b, nb, BLOCK, N_KV_HEADS, HEAD_DIM)
    kk = jnp.concatenate([jnp.concatenate([jnp.zeros_like(kb[:, :1]), kb[:, :-1]], axis=1), kb], axis=2)
    vv = jnp.concatenate([jnp.concatenate([jnp.zeros_like(vb[:, :1]), vb[:, :-1]], axis=1), vb], axis=2)
    q_pos = jnp.arange(s_len, dtype=jnp.int32).reshape(nb, BLOCK)
    k_pos = q_pos[:, :1] - BLOCK + jnp.arange(2 * BLOCK, dtype=jnp.int32)[None, :]
    d = q_pos[:, :, None] - k_pos[:, None, :]
    mask = (d >= 0) & (d <= WINDOW) & (k_pos[:, None, :] >= 0)
    o = _sink_attention(qb, kk, vv, mask[:, None, None], sinks_l)
    return o.reshape(b, s_len, N_HEADS, HEAD_DIM)


def _swa_sample(q, k_all, v_all, q_pos, k_pos, sinks_l):
    db, t = q.shape[:2]
    qg = q.reshape(db, t, N_KV_HEADS, GROUP, HEAD_DIM)
    d = q_pos[:, None] - k_pos[None, :]
    mask = (d >= 0) & (d <= WINDOW)
    o = _sink_attention(qg, k_all, v_all, mask[None, None], sinks_l)
    return o.reshape(db, t, N_HEADS, HEAD_DIM)


def setup_inputs(seed: int = 0) -> dict:
    key = jax.random.key(seed)
    ks = jax.random.split(key, 16)
    n_pages = PAST_LEN // PAGE_SIZE
    used = DEC_BATCH * n_pages
    n_pool = used + max(1, used // 4)
    wb = min(WINDOW, PAST_LEN)
    d_in_a = sum(A_COLS)
    d_in_b = sum(B_COLS)
    nrm = jax.random.normal
    f32 = jnp.float32
    return {
        "x_prompt": nrm(ks[0], (BATCH, SEQ, D_MODEL), f32),
        "x_sample": nrm(ks[1], (DEC_BATCH, DEC_SEQ, D_MODEL), f32),
        "cache_k": nrm(ks[2], (N_A_LAYERS, n_pool, PAGE_SIZE, N_KV_HEADS, HEAD_DIM), f32),
        "cache_v": nrm(ks[3], (N_A_LAYERS, n_pool, PAGE_SIZE, N_KV_HEADS, HEAD_DIM), f32),
        "cache_ik": nrm(ks[4], (N_A_LAYERS, n_pool, PAGE_SIZE, IDX_DIM), f32),
        "page_table": jax.random.permutation(ks[5], n_pool)[:used].reshape(DEC_BATCH, n_pages).astype(jnp.int32),
        "state_wk": nrm(ks[6], (N_B_LAYERS, DEC_BATCH, wb, N_KV_HEADS, HEAD_DIM), f32),
        "state_wv": nrm(ks[7], (N_B_LAYERS, DEC_BATCH, wb, N_KV_HEADS, HEAD_DIM), f32),
        "norm_g": 1.0 + 0.01 * nrm(ks[8], (DEPTH, D_MODEL), f32),
        "final_g": 1.0 + 0.01 * nrm(ks[9], (D_MODEL,), f32),
        "w_in_a": nrm(ks[10], (N_A_LAYERS, D_MODEL, d_in_a), f32) * (D_MODEL ** -0.5),
        "w_out_a": nrm(ks[11], (N_A_LAYERS, BRANCH, D_MODEL), f32) * (BRANCH ** -0.5),
        "w_in_b": nrm(ks[12], (N_B_LAYERS, D_MODEL, d_in_b), f32) * (D_MODEL ** -0.5),
        "w_out_b": nrm(ks[13], (N_B_LAYERS, BRANCH, D_MODEL), f32) * (BRANCH ** -0.5),
        "sinks": nrm(ks[14], (N_B_LAYERS, N_HEADS), f32),
    }


def reference(x_prompt, x_sample, cache_k, cache_v, cache_ik, page_table, state_wk, state_wv,
              norm_g, final_g, w_in_a, w_out_a, w_in_b, w_out_b, sinks):
    s_len = x_prompt.shape[1]
    t_len = x_sample.shape[1]
    past = page_table.shape[1] * PAGE_SIZE
    wb = state_wk.shape[2]
    wb_p = min(WINDOW, s_len)
    pos_p = jnp.arange(s_len, dtype=jnp.int32)
    pos_s = past + jnp.arange(t_len, dtype=jnp.int32)
    win_pos = jnp.concatenate([past - wb + jnp.arange(wb, dtype=jnp.int32), pos_s])

    hp, hs = x_prompt, x_sample
    kp_l, vp_l, ikp_l, ks_l, vs_l, iks_l = [], [], [], [], [], []
    wkp_l, wvp_l, wks_l, wvs_l = [], [], [], []
    for i in range(DEPTH):
        g = norm_g[i]
        if i % N_MIXERS == 0:
            la = i // N_MIXERS
            q, k, v, z, iq, ik, iw = _project_a(hp, g, w_in_a[la], pos_p)
            hp = hp + _gated_out(_dsa_prompt(q, k, v, iq, iw, ik), z, w_out_a[la])
            kp_l.append(k); vp_l.append(v); ikp_l.append(ik)
            q, k, v, z, iq, ik, iw = _project_a(hs, g, w_in_a[la], pos_s)
            o = _dsa_sample(q, k, v, iq, iw, ik, cache_k, cache_v, cache_ik, la, page_table, pos_s)
            hs = hs + _gated_out(o, z, w_out_a[la])
            ks_l.append(k); vs_l.append(v); iks_l.append(ik)
        else:
            lb = i // N_MIXERS
            q, k, v, z = _project_b(hp, g, w_in_b[lb], pos_p)
            hp = hp + _gated_out(_swa_prompt(q, k, v, sinks[lb]), z, w_out_b[lb])
            wkp_l.append(k[:, s_len - wb_p:]); wvp_l.append(v[:, s_len - wb_p:])
            q, k, v, z = _project_b(hs, g, w_in_b[lb], pos_s)
            k_all = jnp.concatenate([state_wk[lb], k], axis=1)
            v_all = jnp.concatenate([state_wv[lb], v], axis=1)
            hs = hs + _gated_out(_swa_sample(q, k_all, v_all, pos_s, win_pos, sinks[lb]), z, w_out_b[lb])
            wks_l.append(k_all[:, t_len:]); wvs_l.append(v_all[:, t_len:])

    y_prompt = _rmsnorm(hp, final_g)
    y_sample = _rmsnorm(hs, final_g)
    return (y_prompt, y_sample,
            jnp.stack(kp_l), jnp.stack(vp_l), jnp.stack(ikp_l),
            jnp.stack(ks_l), jnp.stack(vs_l), jnp.stack(iks_l),
            jnp.stack(wkp_l), jnp.stack(wvp_l), jnp.stack(wks_l), jnp.stack(wvs_l))
```

```python
import functools

import jax
import jax.numpy as jnp
from jax import lax
from jax.experimental import pallas as pl
from jax.experimental.pallas import tpu as pltpu

HEAD_DIM = 64
N_HEADS = 16
N_KV_HEADS = 4
GROUP = N_HEADS // N_KV_HEADS
BRANCH = N_HEADS * HEAD_DIM
KV_WIDTH = N_KV_HEADS * HEAD_DIM
IDX_HEADS = 8
IDX_DIM = 64
TOPK_MAX = 256
WINDOW = 128
BLOCK = 128
PAGE_SIZE = 128
ROPE_THETA = 10000.0
RMS_EPS = 1e-6
N_MIXERS = 2

LANES = 128
KEY_CHUNK = 512
PAGES_PER_STEP = 8
NEG = -1e30
INT_MIN = -(2 ** 31)
FLT_MAX = 3.4028234663852886e38
VMEM_LIMIT = 56 * 1024 * 1024

A_PAD_COLS = 3200
OFF_Q, OFF_K, OFF_V, OFF_Z, OFF_IQ, OFF_IK, OFF_IW = 0, 1024, 1280, 1536, 2560, 3072, 3136

F32 = jnp.float32
BF16 = jnp.bfloat16
I32 = jnp.int32
NT_DIMS = (((1,), (1,)), ((), ()))


def _params(n_axes):
    return pltpu.CompilerParams(dimension_semantics=("arbitrary",) * n_axes,
                                vmem_limit_bytes=VMEM_LIMIT)


def _rope_tables(pos):
    half = HEAD_DIM // 2
    inv = ROPE_THETA ** (-jnp.arange(half, dtype=F32) * 2.0 / HEAD_DIM)
    ang = pos.astype(F32)[:, None] * inv[None, :]
    cos, sin = jnp.cos(ang), jnp.sin(ang)
    return (jnp.concatenate([cos, cos, cos, cos], axis=1),
            jnp.concatenate([-sin, sin, -sin, sin], axis=1))


def _project_kernel(h_ref, g_ref, w_ref, cos_ref, sin_ref, *out_refs, kind):
    if kind == "a":
        (qh_ref, k32_ref, v32_ref, kt_ref, vh_ref, gate_ref,
         iqh_ref, ik32_ref, ikt_ref, iw_ref) = out_refs
    else:
        qh_ref, k32_ref, v32_ref, kt_ref, vh_ref, gate_ref = out_refs
    tm = h_ref.shape[0]
    x = h_ref[...]
    var = jnp.mean(x * x, axis=-1, keepdims=True)
    y = (x * lax.rsqrt(var + RMS_EPS) * g_ref[...]).astype(BF16)
    cos = cos_ref[...]
    sin = sin_ref[...]
    lane = lax.broadcasted_iota(I32, (tm, LANES), 1)
    is_lo = (lane & (HEAD_DIM // 2)) == 0

    def rope(u):
        partner = jnp.where(is_lo, pltpu.roll(u, LANES - HEAD_DIM // 2, 1),
                            pltpu.roll(u, HEAD_DIM // 2, 1))
        return u * cos + partner * sin

    def proj(off, width):
        return jnp.dot(y, w_ref[:, off:off + width], preferred_element_type=F32)

    scale = HEAD_DIM ** -0.5
    uq = proj(OFF_Q, BRANCH)
    for c in range(BRANCH // LANES):
        r = (rope(uq[:, c * LANES:(c + 1) * LANES]) * scale).astype(BF16)
        qh_ref[2 * c] = r[:, :HEAD_DIM]
        qh_ref[2 * c + 1] = r[:, HEAD_DIM:]
    uk = proj(OFF_K, KV_WIDTH)
    uv = proj(OFF_V, KV_WIDTH)
    for c in range(KV_WIDTH // LANES):
        r = rope(uk[:, c * LANES:(c + 1) * LANES])
        k32_ref[:, c * LANES:(c + 1) * LANES] = r
        kt_ref[c * LANES:(c + 1) * LANES, :] = r.T.astype(BF16)
        vv = uv[:, c * LANES:(c + 1) * LANES]
        v32_ref[:, c * LANES:(c + 1) * LANES] = vv
        vb = vv.astype(BF16)
        vh_ref[2 * c] = vb[:, :HEAD_DIM]
        vh_ref[2 * c + 1] = vb[:, HEAD_DIM:]
    uz = proj(OFF_Z, BRANCH)
    gate_ref[...] = uz * (1.0 / (1.0 + jnp.exp(-uz)))
    if kind == "a":
        iscale = IDX_DIM ** -0.5
        ui = proj(OFF_IQ, IDX_HEADS * IDX_DIM)
        for c in range(IDX_HEADS * IDX_DIM // LANES):
            r = (rope(ui[:, c * LANES:(c + 1) * LANES]) * iscale).astype(BF16)
            iqh_ref[2 * c] = r[:, :IDX_DIM]
            iqh_ref[2 * c + 1] = r[:, IDX_DIM:]
        ut = proj(OFF_IK, LANES)
        r = rope(ut)
        ik32_ref[...] = r[:, :IDX_DIM]
        ikt_ref[...] = r.T[:IDX_DIM, :].astype(BF16)
        iw_ref[...] = ut[:, OFF_IW - OFF_IK:OFF_IW - OFF_IK + IDX_HEADS] * (IDX_HEADS ** -0.5)


def _project(h, g, w, cos_t, sin_t, kind):
    m, d = h.shape
    tm = 512 if m % 512 == 0 and m > 512 else 256
    assert m % tm == 0
    wcols = w.shape[1]
    row = lambda i: (i, 0)
    const = lambda i: (0, 0)
    out_shape = [
        jax.ShapeDtypeStruct((N_HEADS, m, HEAD_DIM), BF16),
        jax.ShapeDtypeStruct((m, KV_WIDTH), F32),
        jax.ShapeDtypeStruct((m, KV_WIDTH), F32),
        jax.ShapeDtypeStruct((KV_WIDTH, m), BF16),
        jax.ShapeDtypeStruct((N_KV_HEADS, m, HEAD_DIM), BF16),
        jax.ShapeDtypeStruct((m, BRANCH), F32),
    ]
    out_specs = [
        pl.BlockSpec((N_HEADS, tm, HEAD_DIM), lambda i: (0, i, 0)),
        pl.BlockSpec((tm, KV_WIDTH), row),
        pl.BlockSpec((tm, KV_WIDTH), row),
        pl.BlockSpec((KV_WIDTH, tm), lambda i: (0, i)),
        pl.BlockSpec((N_KV_HEADS, tm, HEAD_DIM), lambda i: (0, i, 0)),
        pl.BlockSpec((tm, BRANCH), row),
    ]
    if kind == "a":
        out_shape += [
            jax.ShapeDtypeStruct((IDX_HEADS, m, IDX_DIM), BF16),
            jax.ShapeDtypeStruct((m, IDX_DIM), F32),
            jax.ShapeDtypeStruct((IDX_DIM, m), BF16),
            jax.ShapeDtypeStruct((m, IDX_HEADS), F32),
        ]
        out_specs += [
            pl.BlockSpec((IDX_HEADS, tm, IDX_DIM), lambda i: (0, i, 0)),
            pl.BlockSpec((tm, IDX_DIM), row),
            pl.BlockSpec((IDX_DIM, tm), lambda i: (0, i)),
            pl.BlockSpec((tm, IDX_HEADS), row),
        ]
    return pl.pallas_call(
        functools.partial(_project_kernel, kind=kind),
        grid=(m // tm,),
        in_specs=[pl.BlockSpec((tm, d), row), pl.BlockSpec((1, d), const),
                  pl.BlockSpec((d, wcols), const),
                  pl.BlockSpec((tm, LANES), row), pl.BlockSpec((tm, LANES), row)],
        out_specs=out_specs,
        out_shape=out_shape,
        compiler_params=_params(1),
        name=f"project_{kind}",
    )(h, g, w, cos_t, sin_t)


def _out_kernel(o_ref, gate_ref, w_ref, h_ref, fg_ref, y_ref, *, final):
    x = (o_ref[...] * gate_ref[...]).astype(BF16)
    y = h_ref[...] + jnp.dot(x, w_ref[...], preferred_element_type=F32)
    if final:
        var = jnp.mean(y * y, axis=-1, keepdims=True)
        y = y * lax.rsqrt(var + RMS_EPS) * fg_ref[...]
    y_ref[...] = y


def _gated_out(o, gate, w, h, fg, final):
    m, d = h.shape
    tm = 512 if m % 512 == 0 and m > 512 else 256
    row = lambda i: (i, 0)
    const = lambda i: (0, 0)
    return pl.pallas_call(
        functools.partial(_out_kernel, final=final),
        grid=(m // tm,),
        in_specs=[pl.BlockSpec((tm, BRANCH), row), pl.BlockSpec((tm, BRANCH), row),
                  pl.BlockSpec((BRANCH, d), const), pl.BlockSpec((tm, d), row),
                  pl.BlockSpec((1, d), const)],
        out_specs=pl.BlockSpec((tm, d), row),
        out_shape=jax.ShapeDtypeStruct((m, d), F32),
        compiler_params=_params(1),
        name="gated_out_final" if final else "gated_out",
    )(o, gate, w, h, fg)


def _key_to_float(key):
    bits = key ^ ((key >> 31) & 0x7FFFFFFF)
    return lax.bitcast_convert_type(bits, F32)


def _select_bias(sc_ref, bias_ref, x_ref, n_chunks, chunk, n_keep):
    rows = sc_ref.shape[0]
    sub = chunk // LANES

    def fold(x):
        acc = x[:, 0:LANES]
        for j in range(1, sub):
            acc = acc + x[:, j * LANES:(j + 1) * LANES]
        return acc

    def count(pred):
        def body(c, acc):
            off = pl.multiple_of(c * chunk, chunk)
            blk = sc_ref[:, pl.ds(off, chunk)]
            idx = off + lax.broadcasted_iota(I32, (rows, chunk), 1)
            return acc + fold(jnp.where(pred(blk, idx), 1, 0).astype(I32))
        acc = lax.fori_loop(0, n_chunks, body, jnp.zeros((rows, LANES), I32))
        return jnp.sum(acc, axis=1, keepdims=True)

    def bisect(j, carry):
        key, cge = carry
        cand = key ^ lax.shift_left(jnp.int32(1), 31 - j)
        cf = _key_to_float(cand)
        c = count(lambda blk, idx: blk >= cf)
        ok = c >= n_keep
        return jnp.where(ok, cand, key), jnp.where(ok, c, cge)

    key, cge = lax.fori_loop(
        0, 32, bisect, (jnp.full((rows, 1), INT_MIN, I32), jnp.zeros((rows, 1), I32)))
    short = key == INT_MIN
    thr = jnp.where(short, -jnp.inf, _key_to_float(key))
    cgt = count(lambda blk, idx: blk > thr)
    need = jnp.where(short, 0, n_keep - cgt)
    ties = jnp.where(short, 0, cge - cgt)
    x_ref[...] = jnp.where(need > 0, jnp.int32(2 ** 30), jnp.int32(-1))
    overflow = jnp.max(jnp.where(ties > need, 1, 0)) > 0

    @pl.when(overflow)
    def _():
        total_bits = max(1, (sc_ref.shape[1] - 1).bit_length())

        def ibisect(j, x):
            cand = x | lax.shift_left(jnp.int32(1), total_bits - 1 - j)
            c = count(lambda blk, idx: (blk == thr) & (idx < cand))
            return jnp.where(c < need, cand, x)

        x = lax.fori_loop(0, total_bits, ibisect, jnp.zeros((rows, 1), I32))
        x_ref[...] = jnp.where(need > 0, x, -1)

    xk = x_ref[...]

    def write(c, _):
        off = pl.multiple_of(c * chunk, chunk)
        blk = sc_ref[:, pl.ds(off, chunk)]
        idx = off + lax.broadcasted_iota(I32, (rows, chunk), 1)
        keep = (blk > thr) | ((blk == thr) & (idx <= xk))
        bias_ref[:, pl.ds(off, chunk)] = jnp.where(keep, 0.0, NEG).astype(F32)
        return 0

    lax.fori_loop(0, n_chunks, write, 0)


def _dsa_prompt_kernel(qh_ref, iqh_ref, iw_ref, ikt_ref, kt_ref, vh_ref, o_ref,
                       sc_ref, bias_ref, wb_ref, x_ref, *, n_keep):
    i = pl.program_id(1)
    n_chunks = (i * BLOCK + BLOCK + KEY_CHUNK - 1) // KEY_CHUNK
    q_pos = i * BLOCK + lax.broadcasted_iota(I32, (BLOCK, KEY_CHUNK), 0)

    for h in range(IDX_HEADS):
        wb_ref[h] = jnp.broadcast_to(iw_ref[:, h:h + 1], (BLOCK, LANES))
    iq = iqh_ref[...].reshape(IDX_HEADS * BLOCK, IDX_DIM)

    def score_chunk(c, _):
        off = pl.multiple_of(c * KEY_CHUNK, KEY_CHUNK)
        s = jnp.dot(iq, ikt_ref[:, pl.ds(off, KEY_CHUNK)], preferred_element_type=F32)
        parts = []
        for j in range(KEY_CHUNK // LANES):
            acc = None
            for h in range(IDX_HEADS):
                term = wb_ref[h] * jnp.maximum(
                    s[h * BLOCK:(h + 1) * BLOCK, j * LANES:(j + 1) * LANES], 0.0)
                acc = term if acc is None else acc + term
            parts.append(acc)
        score = jnp.concatenate(parts, axis=1)
        k_pos = off + lax.broadcasted_iota(I32, (BLOCK, KEY_CHUNK), 1)
        sc_ref[:, pl.ds(off, KEY_CHUNK)] = jnp.where(k_pos <= q_pos, score, -jnp.inf)
        return 0

    lax.fori_loop(0, n_chunks, score_chunk, 0)
    _select_bias(sc_ref, bias_ref, x_ref, n_chunks, KEY_CHUNK, n_keep)

    rows = GROUP * BLOCK
    for g in range(N_KV_HEADS):
        qg = qh_ref[g * GROUP:(g + 1) * GROUP].reshape(rows, HEAD_DIM)

        def attend(c, carry, g=g, qg=qg):
            m, l, acc = carry
            off = pl.multiple_of(c * KEY_CHUNK, KEY_CHUNK)
            s = jnp.dot(qg, kt_ref[g * HEAD_DIM:(g + 1) * HEAD_DIM, pl.ds(off, KEY_CHUNK)],
                        preferred_element_type=F32)
            b = bias_ref[:, pl.ds(off, KEY_CHUNK)]
            s = (s.reshape(GROUP, BLOCK, KEY_CHUNK) + b[None]).reshape(rows, KEY_CHUNK)
            m_new = jnp.maximum(m, jnp.max(s, axis=1, keepdims=True))
            alpha = jnp.exp(m - m_new)
            p = jnp.exp(s - m_new)
            l = alpha * l + jnp.sum(p, axis=1, keepdims=True)
            pv = jnp.dot(p.astype(BF16), vh_ref[g, pl.ds(off, KEY_CHUNK), :],
                         preferred_element_type=F32)
            return m_new, l, alpha * acc + pv

        m, l, acc = lax.fori_loop(
            0, n_chunks, attend,
            (jnp.full((rows, 1), NEG, F32), jnp.zeros((rows, 1), F32),
             jnp.zeros((rows, HEAD_DIM), F32)))
        og = acc / l
        for j in range(GROUP // 2):
            pair = jnp.concatenate([og[(2 * j) * BLOCK:(2 * j + 1) * BLOCK],
                                    og[(2 * j + 1) * BLOCK:(2 * j + 2) * BLOCK]], axis=1)
            col = g * GROUP * HEAD_DIM + j * LANES
            o_ref[:, col:col + LANES] = pair


def _dsa_prompt(qh, iqh, iw, ikt, kt, vh, batch, s_len):
    nb = s_len // BLOCK
    n_keep = min(TOPK_MAX, s_len // 4)
    assert s_len % KEY_CHUNK == 0
    m = batch * s_len
    blk3 = lambda b, i: (0, b * nb + i, 0)
    return pl.pallas_call(
        functools.partial(_dsa_prompt_kernel, n_keep=n_keep),
        grid=(batch, nb),
        in_specs=[pl.BlockSpec((N_HEADS, BLOCK, HEAD_DIM), blk3),
                  pl.BlockSpec((IDX_HEADS, BLOCK, IDX_DIM), blk3),
                  pl.BlockSpec((BLOCK, IDX_HEADS), lambda b, i: (b * nb + i, 0)),
                  pl.BlockSpec((IDX_DIM, s_len), lambda b, i: (0, b)),
                  pl.BlockSpec((KV_WIDTH, s_len), lambda b, i: (0, b)),
                  pl.BlockSpec((N_KV_HEADS, s_len, HEAD_DIM), lambda b, i: (0, b, 0))],
        out_specs=pl.BlockSpec((BLOCK, BRANCH), lambda b, i: (b * nb + i, 0)),
        out_shape=jax.ShapeDtypeStruct((m, BRANCH), F32),
        scratch_shapes=[pltpu.VMEM((BLOCK, s_len), F32), pltpu.VMEM((BLOCK, s_len), F32),
                        pltpu.VMEM((IDX_HEADS, BLOCK, LANES), F32), pltpu.VMEM((BLOCK, 1), I32)],
        compiler_params=_params(2),
        name="dsa_prompt",
    )(qh, iqh, iw, ikt, kt, vh)


def _swa_prompt_kernel(sink_ref, qh_ref, ktp_ref, ktc_ref, vp_ref, vc_ref, o_ref):
    i = pl.program_id(1)
    rows = GROUP * BLOCK
    t = lax.broadcasted_iota(I32, (BLOCK, BLOCK), 0)
    j = lax.broadcasted_iota(I32, (BLOCK, BLOCK), 1)
    d_prev = BLOCK + t - j
    d_cur = t - j
    ok_prev = (d_prev >= 0) & (d_prev <= WINDOW) & (i > 0)
    ok_cur = (d_cur >= 0) & (d_cur <= WINDOW)
    b_prev = jnp.where(ok_prev, 0.0, NEG).astype(F32)
    b_cur = jnp.where(ok_cur, 0.0, NEG).astype(F32)
    for g in range(N_KV_HEADS):
        qg = qh_ref[g * GROUP:(g + 1) * GROUP].reshape(rows, HEAD_DIM)
        rs = slice(g * HEAD_DIM, (g + 1) * HEAD_DIM)
        sp = jnp.dot(qg, ktp_ref[rs, :], preferred_element_type=F32)
        sc = jnp.dot(qg, ktc_ref[rs, :], preferred_element_type=F32)
        sp = (sp.reshape(GROUP, BLOCK, BLOCK) + b_prev[None]).reshape(rows, BLOCK)
        sc = (sc.reshape(GROUP, BLOCK, BLOCK) + b_cur[None]).reshape(rows, BLOCK)
        sink = jnp.concatenate(
            [jnp.full((BLOCK, 1), sink_ref[g * GROUP + r], F32) for r in range(GROUP)], axis=0)
        m = jnp.maximum(jnp.maximum(jnp.max(sp, axis=1, keepdims=True),
                                    jnp.max(sc, axis=1, keepdims=True)), sink)
        pp = jnp.exp(sp - m)
        pc = jnp.exp(sc - m)
        denom = (jnp.sum(pp, axis=1, keepdims=True) + jnp.sum(pc, axis=1, keepdims=True)
                 + jnp.exp(sink - m))
        inv = 1.0 / denom
        og = (jnp.dot((pp * inv).astype(BF16), vp_ref[g], preferred_element_type=F32)
              + jnp.dot((pc * inv).astype(BF16), vc_ref[g], preferred_element_type=F32))
        for jj in range(GROUP // 2):
            pair = jnp.concatenate([og[(2 * jj) * BLOCK:(2 * jj + 1) * BLOCK],
                                    og[(2 * jj + 1) * BLOCK:(2 * jj + 2) * BLOCK]], axis=1)
            col = g * GROUP * HEAD_DIM + jj * LANES
            o_ref[:, col:col + LANES] = pair


def _swa_prompt(sinks_l, qh, kt, vh, batch, s_len):
    assert WINDOW <= BLOCK
    nb = s_len // BLOCK
    m = batch * s_len
    cur = lambda b, i: b * nb + i
    prev = lambda b, i: b * nb + jnp.maximum(i - 1, 0)
    return pl.pallas_call(
        _swa_prompt_kernel,
        grid=(batch, nb),
        in_specs=[pl.BlockSpec(memory_space=pltpu.SMEM),
                  pl.BlockSpec((N_HEADS, BLOCK, HEAD_DIM), lambda b, i: (0, cur(b, i), 0)),
                  pl.BlockSpec((KV_WIDTH, BLOCK), lambda b, i: (0, prev(b, i))),
                  pl.BlockSpec((KV_WIDTH, BLOCK), lambda b, i: (0, cur(b, i))),
                  pl.BlockSpec((N_KV_HEADS, BLOCK, HEAD_DIM), lambda b, i: (0, prev(b, i), 0)),
                  pl.BlockSpec((N_KV_HEADS, BLOCK, HEAD_DIM), lambda b, i: (0, cur(b, i), 0))],
        out_specs=pl.BlockSpec((BLOCK, BRANCH), lambda b, i: (cur(b, i), 0)),
        out_shape=jax.ShapeDtypeStruct((m, BRANCH), F32),
        compiler_params=_params(2),
        name="swa_prompt",
    )(sinks_l, qh, kt, kt, vh, vh)


def _block_diag_q(qh, n_seq, t_len):
    q = qh.reshape(N_KV_HEADS, GROUP, n_seq, t_len, HEAD_DIM)
    q = jnp.transpose(q, (2, 3, 0, 1, 4))
    eye = jnp.eye(N_KV_HEADS, dtype=q.dtype)
    qbd = q[:, :, :, :, None, :] * eye[None, None, :, None, :, None]
    return qbd.reshape(n_seq, t_len * N_HEADS, KV_WIDTH)


def _block_diag_out(acc, n_seq, t_len):
    a = acc.reshape(n_seq, t_len, N_KV_HEADS, GROUP, N_KV_HEADS, HEAD_DIM)
    idx = jnp.arange(N_KV_HEADS)
    a = a[:, :, idx, :, idx, :]
    return jnp.transpose(a, (1, 2, 0, 3, 4)).reshape(n_seq * t_len, BRANCH)


def _rows_from_tokens(x, t_len):
    return jnp.concatenate(
        [jnp.broadcast_to(x[t:t + 1], (N_HEADS, x.shape[1])) for t in range(t_len)], axis=0)


def _dsa_sample_select_kernel(pt_ref, iq_ref, w_ref, ikn_ref, *rest, t_len, n_keep, past):
    ik_refs = rest[:PAGES_PER_STEP]
    bias_ref, biasn_ref, sc_ref, bsc_ref, x_ref = rest[PAGES_PER_STEP:]
    j = pl.program_id(1)
    n_steps = pl.num_programs(1)
    iq = iq_ref[0]
    wcol = w_ref[0]

    def scores(keys_t):
        s = jnp.dot(iq, keys_t, preferred_element_type=F32)
        s = wcol * jnp.maximum(s, 0.0)
        return jnp.sum(s.reshape(t_len, IDX_HEADS, s.shape[1]), axis=1)

    for r in range(PAGES_PER_STEP):
        off = pl.multiple_of((j * PAGES_PER_STEP + r) * PAGE_SIZE, PAGE_SIZE)
        sc_ref[:, pl.ds(off, PAGE_SIZE)] = scores(ik_refs[r][0, 0].astype(BF16))

    @pl.when(j == n_steps - 1)
    def _():
        sn = scores(ikn_ref[0].astype(BF16))
        tq = lax.broadcasted_iota(I32, (t_len, LANES), 0)
        kn = lax.broadcasted_iota(I32, (t_len, LANES), 1)
        sc_ref[:, past:past + LANES] = jnp.where(kn <= tq, sn, -jnp.inf)
        n_cols = past + LANES
        _select_bias(sc_ref, bsc_ref, x_ref, n_cols // LANES, LANES, n_keep)
        bias_ref[0] = bsc_ref[:, :past]
        biasn_ref[0] = bsc_ref[:, past:]


def _dsa_sample_select(page_table, iq_rows, w_rows, ik_new_t, cache_ik_t, la, t_len):
    n_seq, n_pages = page_table.shape
    past = n_pages * PAGE_SIZE
    assert n_pages % PAGES_PER_STEP == 0 and t_len <= LANES
    n_keep = min(TOPK_MAX, (past + t_len) // 4)
    n_steps = n_pages // PAGES_PER_STEP
    n_cols = past + LANES

    def page_spec(r):
        return pl.BlockSpec((1, 1, IDX_DIM, PAGE_SIZE),
                            lambda b, j, pt: (la, pt[b, j * PAGES_PER_STEP + r], 0, 0))

    seq3 = lambda b, j, pt: (b, 0, 0)
    grid_spec = pltpu.PrefetchScalarGridSpec(
        num_scalar_prefetch=1,
        grid=(n_seq, n_steps),
        in_specs=[pl.BlockSpec((1, t_len * IDX_HEADS, IDX_DIM), seq3),
                  pl.BlockSpec((1, t_len * IDX_HEADS, 1), seq3),
                  pl.BlockSpec((1, IDX_DIM, LANES), seq3)]
                 + [page_spec(r) for r in range(PAGES_PER_STEP)],
        out_specs=[pl.BlockSpec((1, t_len, past), seq3),
                   pl.BlockSpec((1, t_len, LANES), seq3)],
        scratch_shapes=[pltpu.VMEM((t_len, n_cols), F32), pltpu.VMEM((t_len, n_cols), F32),
                        pltpu.VMEM((t_len, 1), I32)],
    )
    return pl.pallas_call(
        functools.partial(_dsa_sample_select_kernel, t_len=t_len, n_keep=n_keep, past=past),
        grid_spec=grid_spec,
        out_shape=[jax.ShapeDtypeStruct((n_seq, t_len, past), F32),
                   jax.ShapeDtypeStruct((n_seq, t_len, LANES), F32)],
        compiler_params=_params(2),
        name="dsa_sample_select",
    )(page_table, iq_rows, w_rows, ik_new_t, *([cache_ik_t] * PAGES_PER_STEP))


def _dsa_sample_attend_kernel(pt_ref, q_ref, bias_ref, biasn_ref, kn_ref, vn_ref, *rest, t_len):
    k_refs = rest[:PAGES_PER_STEP]
    v_refs = rest[PAGES_PER_STEP:2 * PAGES_PER_STEP]
    o_ref, m_ref, l_ref, acc_ref = rest[2 * PAGES_PER_STEP:]
    j = pl.program_id(1)
    n_steps = pl.num_programs(1)
    q = q_ref[0]

    @pl.when(j == 0)
    def _():
        m_ref[...] = jnp.full(m_ref.shape, NEG, F32)
        l_ref[...] = jnp.zeros(l_ref.shape, F32)
        acc_ref[...] = jnp.zeros(acc_ref.shape, F32)

    def update(s, vals_t):
        m = m_ref[...]
        m_new = jnp.maximum(m, jnp.max(s, axis=1, keepdims=True))
        alpha = jnp.exp(m - m_new)
        p = jnp.exp(s - m_new)
        l_ref[...] = alpha * l_ref[...] + jnp.sum(p, axis=1, keepdims=True)
        acc = alpha * acc_ref[...]
        start = 0
        for v in vals_t:
            n = v.shape[1]
            acc = acc + lax.dot_general(p[:, start:start + n].astype(BF16), v, NT_DIMS,
                                        preferred_element_type=F32)
            start += n
        acc_ref[...] = acc
        m_ref[...] = m_new

    bias = _rows_from_tokens(bias_ref[0], t_len)
    s = jnp.concatenate(
        [jnp.dot(q, k_refs[r][0, 0].astype(BF16), preferred_element_type=F32)
         for r in range(PAGES_PER_STEP)], axis=1) + bias
    update(s, [v_refs[r][0, 0].astype(BF16) for r in range(PAGES_PER_STEP)])

    @pl.when(j == n_steps - 1)
    def _():
        bn = _rows_from_tokens(biasn_ref[0], t_len)
        sn = jnp.dot(q, kn_ref[0].astype(BF16), preferred_element_type=F32) + bn
        update(sn, [vn_ref[0].astype(BF16)])
        o_ref[0] = acc_ref[...] / l_ref[...]


def _dsa_sample_attend(page_table, q_bd, bias, bias_new, k_new_t, v_new_t, cache_k_t, cache_v_t,
                       la, t_len):
    n_seq, n_pages = page_table.shape
    n_steps = n_pages // PAGES_PER_STEP
    rows = t_len * N_HEADS
    chunk = PAGES_PER_STEP * PAGE_SIZE

    def page_spec(r):
        return pl.BlockSpec((1, 1, KV_WIDTH, PAGE_SIZE),
                            lambda b, j, pt: (la, pt[b, j * PAGES_PER_STEP + r], 0, 0))

    seq3 = lambda b, j, pt: (b, 0, 0)
    grid_spec = pltpu.PrefetchScalarGridSpec(
        num_scalar_prefetch=1,
        grid=(n_seq, n_steps),
        in_specs=[pl.BlockSpec((1, rows, KV_WIDTH), seq3),
                  pl.BlockSpec((1, t_len, chunk), lambda b, j, pt: (b, 0, j)),
                  pl.BlockSpec((1, t_len, LANES), seq3),
                  pl.BlockSpec((1, KV_WIDTH, LANES), seq3),
                  pl.BlockSpec((1, KV_WIDTH, LANES), seq3)]
                 + [page_spec(r) for r in range(PAGES_PER_STEP)] * 2,
        out_specs=pl.BlockSpec((1, rows, KV_WIDTH), seq3),
        scratch_shapes=[pltpu.VMEM((rows, 1), F32), pltpu.VMEM((rows, 1), F32),
                        pltpu.VMEM((rows, KV_WIDTH), F32)],
    )
    return pl.pallas_call(
        functools.partial(_dsa_sample_attend_kernel, t_len=t_len),
        grid_spec=grid_spec,
        out_shape=jax.ShapeDtypeStruct((n_seq, rows, KV_WIDTH), F32),
        compiler_params=_params(2),
        name="dsa_sample_attend",
    )(page_table, q_bd, bias, bias_new, k_new_t, v_new_t,
      *([cache_k_t] * PAGES_PER_STEP), *([cache_v_t] * PAGES_PER_STEP))


def _swa_sample_kernel(sink_ref, q_ref, kn_ref, vn_ref, wk_ref, wv_ref, o_ref, nwk_ref, nwv_ref,
                       *, t_len, seqs):
    wb = wk_ref.shape[2]
    rows = t_len * N_HEADS
    t_row = lax.broadcasted_iota(I32, (rows, wb), 0) // N_HEADS
    i_col = lax.broadcasted_iota(I32, (rows, wb), 1)
    d_state = wb + t_row - i_col
    b_state = jnp.where((d_state >= 0) & (d_state <= WINDOW), 0.0, NEG).astype(F32)
    t_row_n = lax.broadcasted_iota(I32, (rows, LANES), 0) // N_HEADS
    j_col = lax.broadcasted_iota(I32, (rows, LANES), 1)
    d_new = t_row_n - j_col
    b_new = jnp.where((d_new >= 0) & (d_new <= WINDOW) & (j_col < t_len), 0.0, NEG).astype(F32)
    h_row = lax.broadcasted_iota(I32, (rows, 1), 0) % N_HEADS
    sink = jnp.zeros((rows, 1), F32)
    for h in range(N_HEADS):
        sink = jnp.where(h_row == h, sink_ref[h], sink)
    keep_old = lax.broadcasted_iota(I32, (KV_WIDTH, wb), 1) < wb - t_len
    for s_i in range(seqs):
        q = q_ref[s_i]
        wk = wk_ref[s_i]
        wv = wv_ref[s_i]
        kn = kn_ref[s_i]
        vn = vn_ref[s_i]
        ss = jnp.dot(q, wk.astype(BF16), preferred_element_type=F32) + b_state
        sn = jnp.dot(q, kn.astype(BF16), preferred_element_type=F32) + b_new
        m = jnp.maximum(jnp.maximum(jnp.max(ss, axis=1, keepdims=True),
                                    jnp.max(sn, axis=1, keepdims=True)), sink)
        ps = jnp.exp(ss - m)
        pn = jnp.exp(sn - m)
        denom = (jnp.sum(ps, axis=1, keepdims=True) + jnp.sum(pn, axis=1, keepdims=True)
                 + jnp.exp(sink - m))
        inv = 1.0 / denom
        o_ref[s_i] = (lax.dot_general((ps * inv).astype(BF16), wv.astype(BF16), NT_DIMS,
                                      preferred_element_type=F32)
                      + lax.dot_general((pn * inv).astype(BF16), vn.astype(BF16), NT_DIMS,
                                        preferred_element_type=F32))
        nwk_ref[s_i] = jnp.where(keep_old, pltpu.roll(wk, wb - t_len, 1), pltpu.roll(kn, wb - t_len, 1))
        nwv_ref[s_i] = jnp.where(keep_old, pltpu.roll(wv, wb - t_len, 1), pltpu.roll(vn, wb - t_len, 1))


def _swa_sample(sinks_l, q_bd, k_new_t, v_new_t, state_wk_t, state_wv_t, lb, t_len):
    _, n_seq, _, wb = state_wk_t.shape
    seqs = 8 if n_seq % 8 == 0 else 1
    rows = t_len * N_HEADS
    assert wb == LANES and wb >= t_len
    seq3 = lambda b: (b, 0, 0)
    st4 = lambda b: (lb, b, 0, 0)
    return pl.pallas_call(
        functools.partial(_swa_sample_kernel, t_len=t_len, seqs=seqs),
        grid=(n_seq // seqs,),
        in_specs=[pl.BlockSpec(memory_space=pltpu.SMEM),
                  pl.BlockSpec((seqs, rows, KV_WIDTH), seq3),
                  pl.BlockSpec((seqs, KV_WIDTH, LANES), seq3),
                  pl.BlockSpec((seqs, KV_WIDTH, LANES), seq3),
                  pl.BlockSpec((None, seqs, KV_WIDTH, wb), st4),
                  pl.BlockSpec((None, seqs, KV_WIDTH, wb), st4)],
        out_specs=[pl.BlockSpec((seqs, rows, KV_WIDTH), seq3),
                   pl.BlockSpec((seqs, KV_WIDTH, wb), seq3),
                   pl.BlockSpec((seqs, KV_WIDTH, wb), seq3)],
        out_shape=[jax.ShapeDtypeStruct((n_seq, rows, KV_WIDTH), F32),
                   jax.ShapeDtypeStruct((n_seq, KV_WIDTH, wb), F32),
                   jax.ShapeDtypeStruct((n_seq, KV_WIDTH, wb), F32)],
        compiler_params=_params(1),
        name="swa_sample",
    )(sinks_l, q_bd, k_new_t, v_new_t, state_wk_t, state_wv_t)


def _new_rows_t(x, n_seq, t_len):
    xt = jnp.transpose(x.reshape(n_seq, t_len, x.shape[1]), (0, 2, 1))
    return jnp.pad(xt, ((0, 0), (0, 0), (0, LANES - t_len)))


def kernel(x_prompt, x_sample, cache_k, cache_v, cache_ik, page_table, state_wk, state_wv,
           norm_g, final_g, w_in_a, w_out_a, w_in_b, w_out_b, sinks):
    batch, s_len, d = x_prompt.shape
    n_seq, t_len, _ = x_sample.shape
    depth = norm_g.shape[0]
    n_pool = cache_k.shape[1]
    past = page_table.shape[1] * PAGE_SIZE
    wb = state_wk.shape[2]
    wb_p = min(WINDOW, s_len)

    cos_p, sin_p = _rope_tables(jnp.tile(jnp.arange(s_len, dtype=I32), batch))
    cos_s, sin_s = _rope_tables(jnp.tile(past + jnp.arange(t_len, dtype=I32), n_seq))
    ck_t = jnp.transpose(cache_k, (0, 1, 3, 4, 2)).reshape(cache_k.shape[0], n_pool, KV_WIDTH, PAGE_SIZE)
    cv_t = jnp.transpose(cache_v, (0, 1, 3, 4, 2)).reshape(cache_v.shape[0], n_pool, KV_WIDTH, PAGE_SIZE)
    cik_t = jnp.transpose(cache_ik, (0, 1, 3, 2))
    swk_t = jnp.transpose(state_wk, (0, 1, 3, 4, 2)).reshape(state_wk.shape[0], n_seq, KV_WIDTH, wb)
    swv_t = jnp.transpose(state_wv, (0, 1, 3, 4, 2)).reshape(state_wv.shape[0], n_seq, KV_WIDTH, wb)
    fg = final_g.reshape(1, d)

    hp = x_prompt.reshape(batch * s_len, d)
    hs = x_sample.reshape(n_seq * t_len, d)
    kp_l, vp_l, ikp_l, ks_l, vs_l, iks_l = [], [], [], [], [], []
    wkp_l, wvp_l, wks_l, wvs_l = [], [], [], []
    for i in range(depth):
        g = norm_g[i].reshape(1, d)
        final = i == depth - 1
        if i % N_MIXERS == 0:
            la = i // N_MIXERS
            w_in = jnp.pad(w_in_a[la], ((0, 0), (0, A_PAD_COLS - w_in_a.shape[2]))).astype(BF16)
            w_out = w_out_a[la].astype(BF16)
            qh, k32, v32, kt, vh, gate, iqh, ik32, ikt, iw = _project(hp, g, w_in, cos_p, sin_p, "a")
            o = _dsa_prompt(qh, iqh, iw, ikt, kt, vh, batch, s_len)
            hp = _gated_out(o, gate, w_out, hp, fg, final)
            kp_l.append(k32.reshape(batch, s_len, N_KV_HEADS, HEAD_DIM))
            vp_l.append(v32.reshape(batch, s_len, N_KV_HEADS, HEAD_DIM))
            ikp_l.append(ik32.reshape(batch, s_len, IDX_DIM))
            qh, k32, v32, kt, vh, gate, iqh, ik32, ikt, iw = _project(hs, g, w_in, cos_s, sin_s, "a")
            iq_rows = jnp.transpose(iqh.reshape(IDX_HEADS, n_seq, t_len, IDX_DIM),
                                    (1, 2, 0, 3)).reshape(n_seq, t_len * IDX_HEADS, IDX_DIM)
            w_rows = iw.reshape(n_seq, t_len * IDX_HEADS, 1)
            bias, bias_new = _dsa_sample_select(page_table, iq_rows, w_rows,
                                                _new_rows_t(ik32, n_seq, t_len), cik_t, la, t_len)
            acc = _dsa_sample_attend(page_table, _block_diag_q(qh, n_seq, t_len), bias, bias_new,
                                     _new_rows_t(k32, n_seq, t_len), _new_rows_t(v32, n_seq, t_len),
                                     ck_t, cv_t, la, t_len)
            hs = _gated_out(_block_diag_out(acc, n_seq, t_len), gate, w_out, hs, fg, final)
            ks_l.append(k32.reshape(n_seq, t_len, N_KV_HEADS, HEAD_DIM))
            vs_l.append(v32.reshape(n_seq, t_len, N_KV_HEADS, HEAD_DIM))
            iks_l.append(ik32.reshape(n_seq, t_len, IDX_DIM))
        else:
            lb = i // N_MIXERS
            w_in = w_in_b[lb].astype(BF16)
            w_out = w_out_b[lb].astype(BF16)
            qh, k32, v32, kt, vh, gate = _project(hp, g, w_in, cos_p, sin_p, "b")
            o = _swa_prompt(sinks[lb], qh, kt, vh, batch, s_len)
            hp = _gated_out(o, gate, w_out, hp, fg, final)
            k4 = k32.reshape(batch, s_len, N_KV_HEADS, HEAD_DIM)
            v4 = v32.reshape(batch, s_len, N_KV_HEADS, HEAD_DIM)
            wkp_l.append(k4[:, s_len - wb_p:])
            wvp_l.append(v4[:, s_len - wb_p:])
            qh, k32, v32, kt, vh, gate = _project(hs, g, w_in, cos_s, sin_s, "b")
            acc, nwk, nwv = _swa_sample(sinks[lb], _block_diag_q(qh, n_seq, t_len),
                                        _new_rows_t(k32, n_seq, t_len), _new_rows_t(v32, n_seq, t_len),
                                        swk_t, swv_t, lb, t_len)
            hs = _gated_out(_block_diag_out(acc, n_seq, t_len), gate, w_out, hs, fg, final)
            back = lambda a: jnp.transpose(a.reshape(n_seq, N_KV_HEADS, HEAD_DIM, wb), (0, 3, 1, 2))
            wks_l.append(back(nwk))
            wvs_l.append(back(nwv))

    assert depth >= 1
    return (hp.reshape(batch, s_len, d), hs.reshape(n_seq, t_len, d),
            jnp.stack(kp_l), jnp.stack(vp_l), jnp.stack(ikp_l),
            jnp.stack(ks_l), jnp.stack(vs_l), jnp.stack(iks_l),
            jnp.stack(wkp_l), jnp.stack(wvp_l), jnp.stack(wks_l), jnp.stack(wvs_l))
```

```python
import functools

import jax
import jax.numpy as jnp
from jax import lax
from jax.experimental import pallas as pl
from jax.experimental.pallas import tpu as pltpu

HEAD_DIM = 64
N_HEADS = 16
N_KV_HEADS = 4
GROUP = N_HEADS // N_KV_HEADS
BRANCH = N_HEADS * HEAD_DIM
KV_WIDTH = N_KV_HEADS * HEAD_DIM
IDX_HEADS = 8
IDX_DIM = 64
TOPK_MAX = 256
WINDOW = 128
BLOCK = 128
PAGE_SIZE = 128
ROPE_THETA = 10000.0
RMS_EPS = 1e-6
N_MIXERS = 2

LANES = 128
SUBLANES = 8
KEY_CHUNK = 512
ATTN_UNROLL = 2
PAGES_PER_STEP = 8
NEG = -1e30
LOG2E = 1.4426950408889634
Q_SCALE = HEAD_DIM ** -0.5 * LOG2E
INT_MIN = -(2 ** 31)
FLT_MAX = 3.4028234663852886e38
VMEM_LIMIT = 56 * 1024 * 1024

A_PAD_COLS = 3200
OFF_Q, OFF_K, OFF_V, OFF_Z, OFF_IQ, OFF_IK, OFF_IW = 0, 1024, 1280, 1536, 2560, 3072, 3136

F32 = jnp.float32
BF16 = jnp.bfloat16
I32 = jnp.int32
NT_DIMS = (((1,), (1,)), ((), ()))


def _params(n_axes):
    return pltpu.CompilerParams(dimension_semantics=("arbitrary",) * n_axes,
                                vmem_limit_bytes=VMEM_LIMIT)


def _rope_tables(pos):
    half = HEAD_DIM // 2
    inv = ROPE_THETA ** (-jnp.arange(half, dtype=F32) * 2.0 / HEAD_DIM)
    ang = pos.astype(F32)[:, None] * inv[None, :]
    cos, sin = jnp.cos(ang), jnp.sin(ang)
    return (jnp.concatenate([cos, cos, cos, cos], axis=1),
            jnp.concatenate([-sin, sin, -sin, sin], axis=1))


def _project_kernel(h_ref, g_ref, w_ref, cos_ref, sin_ref, *out_refs, kind):
    if kind == "a":
        (qh_ref, k32_ref, v32_ref, kt_ref, vx_ref, gate_ref,
         iqh_ref, ik32_ref, ikt_ref, iw_ref) = out_refs
    else:
        qh_ref, k32_ref, v32_ref, kt_ref, vx_ref, gate_ref = out_refs
    tm = h_ref.shape[0]
    x = h_ref[...]
    var = jnp.mean(x * x, axis=-1, keepdims=True)
    y = (x * lax.rsqrt(var + RMS_EPS) * g_ref[...]).astype(BF16)
    cos = cos_ref[...]
    sin = sin_ref[...]
    lane = lax.broadcasted_iota(I32, (tm, LANES), 1)
    is_lo = (lane & (HEAD_DIM // 2)) == 0

    def rope(u):
        partner = jnp.where(is_lo, pltpu.roll(u, LANES - HEAD_DIM // 2, 1),
                            pltpu.roll(u, HEAD_DIM // 2, 1))
        return u * cos + partner * sin

    def proj(off, width):
        return jnp.dot(y, w_ref[:, off:off + width], preferred_element_type=F32)

    scale = Q_SCALE
    ones_col = jnp.where(lane == HEAD_DIM, 1.0, 0.0).astype(F32)
    uq = proj(OFF_Q, BRANCH)
    for c in range(BRANCH // LANES):
        r = (rope(uq[:, c * LANES:(c + 1) * LANES]) * scale).astype(BF16)
        qh_ref[2 * c] = r[:, :HEAD_DIM]
        qh_ref[2 * c + 1] = r[:, HEAD_DIM:]
    uk = proj(OFF_K, KV_WIDTH)
    uv = proj(OFF_V, KV_WIDTH)
    for c in range(KV_WIDTH // LANES):
        r = rope(uk[:, c * LANES:(c + 1) * LANES])
        k32_ref[:, c * LANES:(c + 1) * LANES] = r
        kt_ref[c * LANES:(c + 1) * LANES, :] = r.T.astype(BF16)
        vv = uv[:, c * LANES:(c + 1) * LANES]
        v32_ref[:, c * LANES:(c + 1) * LANES] = vv
        vx_ref[2 * c] = jnp.where(lane < HEAD_DIM, vv, ones_col).astype(BF16)
        vx_ref[2 * c + 1] = jnp.where(lane < HEAD_DIM, pltpu.roll(vv, HEAD_DIM, 1), ones_col).astype(BF16)
    uz = proj(OFF_Z, BRANCH)
    gate_ref[...] = uz * (1.0 / (1.0 + jnp.exp(-uz)))
    if kind == "a":
        iscale = IDX_DIM ** -0.5
        ui = proj(OFF_IQ, IDX_HEADS * IDX_DIM)
        for c in range(IDX_HEADS * IDX_DIM // LANES):
            r = (rope(ui[:, c * LANES:(c + 1) * LANES]) * iscale).astype(BF16)
            iqh_ref[2 * c] = r[:, :IDX_DIM]
            iqh_ref[2 * c + 1] = r[:, IDX_DIM:]
        ut = proj(OFF_IK, LANES)
        r = rope(ut)
        ik32_ref[...] = r[:, :IDX_DIM]
        ikt_ref[...] = r.T[:IDX_DIM, :].astype(BF16)
        iw_ref[...] = ut[:, OFF_IW - OFF_IK:OFF_IW - OFF_IK + IDX_HEADS] * (IDX_HEADS ** -0.5)


def _project(h, g, w, cos_t, sin_t, kind):
    m, d = h.shape
    tm = 512 if m % 512 == 0 and m > 512 else 256
    assert m % tm == 0
    wcols = w.shape[1]
    row = lambda i: (i, 0)
    const = lambda i: (0, 0)
    out_shape = [
        jax.ShapeDtypeStruct((N_HEADS, m, HEAD_DIM), BF16),
        jax.ShapeDtypeStruct((m, KV_WIDTH), F32),
        jax.ShapeDtypeStruct((m, KV_WIDTH), F32),
        jax.ShapeDtypeStruct((KV_WIDTH, m), BF16),
        jax.ShapeDtypeStruct((N_KV_HEADS, m, LANES), BF16),
        jax.ShapeDtypeStruct((m, BRANCH), F32),
    ]
    out_specs = [
        pl.BlockSpec((N_HEADS, tm, HEAD_DIM), lambda i: (0, i, 0)),
        pl.BlockSpec((tm, KV_WIDTH), row),
        pl.BlockSpec((tm, KV_WIDTH), row),
        pl.BlockSpec((KV_WIDTH, tm), lambda i: (0, i)),
        pl.BlockSpec((N_KV_HEADS, tm, LANES), lambda i: (0, i, 0)),
        pl.BlockSpec((tm, BRANCH), row),
    ]
    if kind == "a":
        out_shape += [
            jax.ShapeDtypeStruct((IDX_HEADS, m, IDX_DIM), BF16),
            jax.ShapeDtypeStruct((m, IDX_DIM), F32),
            jax.ShapeDtypeStruct((IDX_DIM, m), BF16),
            jax.ShapeDtypeStruct((m, IDX_HEADS), F32),
        ]
        out_specs += [
            pl.BlockSpec((IDX_HEADS, tm, IDX_DIM), lambda i: (0, i, 0)),
            pl.BlockSpec((tm, IDX_DIM), row),
            pl.BlockSpec((IDX_DIM, tm), lambda i: (0, i)),
            pl.BlockSpec((tm, IDX_HEADS), row),
        ]
    return pl.pallas_call(
        functools.partial(_project_kernel, kind=kind),
        grid=(m // tm,),
        in_specs=[pl.BlockSpec((tm, d), row), pl.BlockSpec((1, d), const),
                  pl.BlockSpec((d, wcols), const),
                  pl.BlockSpec((tm, LANES), row), pl.BlockSpec((tm, LANES), row)],
        out_specs=out_specs,
        out_shape=out_shape,
        compiler_params=_params(1),
        name=f"project_{kind}",
    )(h, g, w, cos_t, sin_t)


def _out_kernel(o_ref, gate_ref, w_ref, h_ref, fg_ref, y_ref, *, final):
    x = (o_ref[...] * gate_ref[...]).astype(BF16)
    y = h_ref[...] + jnp.dot(x, w_ref[...], preferred_element_type=F32)
    if final:
        var = jnp.mean(y * y, axis=-1, keepdims=True)
        y = y * lax.rsqrt(var + RMS_EPS) * fg_ref[...]
    y_ref[...] = y


def _gated_out(o, gate, w, h, fg, final):
    m, d = h.shape
    tm = 512 if m % 512 == 0 and m > 512 else 256
    row = lambda i: (i, 0)
    const = lambda i: (0, 0)
    return pl.pallas_call(
        functools.partial(_out_kernel, final=final),
        grid=(m // tm,),
        in_specs=[pl.BlockSpec((tm, BRANCH), row), pl.BlockSpec((tm, BRANCH), row),
                  pl.BlockSpec((BRANCH, d), const), pl.BlockSpec((tm, d), row),
                  pl.BlockSpec((1, d), const)],
        out_specs=pl.BlockSpec((tm, d), row),
        out_shape=jax.ShapeDtypeStruct((m, d), F32),
        compiler_params=_params(1),
        name="gated_out_final" if final else "gated_out",
    )(o, gate, w, h, fg)


def _key_to_float(key):
    bits = key ^ ((key >> 31) & 0x7FFFFFFF)
    return lax.bitcast_convert_type(bits, F32)


def _select_bias(segments, n_keep, n_cols):
    rows = segments[0][0].shape[0]

    def fold(x):
        acc = x[:, 0:LANES]
        for j in range(1, x.shape[1] // LANES):
            acc = acc + x[:, j * LANES:(j + 1) * LANES]
        return acc

    def for_chunks(fn, init):
        carry = init
        for seg in segments:
            sc_ref, _, n_chunks, chunk, first_col = seg

            def body(c, carry, seg=seg, sc_ref=sc_ref, chunk=chunk, first_col=first_col):
                off = pl.multiple_of(c * chunk, chunk)
                blk = sc_ref[:, pl.ds(off, chunk)]
                idx = first_col + off + lax.broadcasted_iota(I32, (rows, chunk), 1)
                return fn(seg, off, blk, idx, carry)
            carry = lax.fori_loop(0, n_chunks, body, carry)
        return carry

    def count(pred):
        acc = for_chunks(
            lambda seg, off, blk, idx, acc: acc + fold(jnp.where(pred(blk, idx), 1, 0).astype(I32)),
            jnp.zeros((rows, LANES), I32))
        return jnp.sum(acc, axis=1, keepdims=True)

    def bisect(carry):
        j, key, cge = carry
        cand = key ^ lax.shift_left(jnp.int32(1), 31 - j)
        cf = _key_to_float(cand)
        c = count(lambda blk, idx: blk >= cf)
        ok = c >= n_keep
        return j + 1, jnp.where(ok, cand, key), jnp.where(ok, c, cge)

    def unsettled(carry):
        j, _, cge = carry
        return (j < 32) & (jnp.max(jnp.where(cge == n_keep, 0, 1)) > 0)

    _, key, cge = lax.while_loop(
        unsettled, bisect,
        (jnp.int32(0), jnp.full((rows, 1), INT_MIN, I32), jnp.zeros((rows, 1), I32)))
    short = key == INT_MIN
    thr = jnp.where(short, -jnp.inf, _key_to_float(key))
    cgt = count(lambda blk, idx: blk > thr)
    need = jnp.where(short, 0, n_keep - cgt)
    ties = jnp.where(short, 0, cge - cgt)
    overflow = jnp.max(jnp.where(ties > need, 1, 0)) > 0

    def write(keep_fn):
        def store(seg, off, blk, idx, carry):
            seg[1][:, pl.ds(off, seg[3])] = jnp.where(keep_fn(blk, idx), 0.0, NEG).astype(F32)
            return carry
        for_chunks(store, 0)

    @pl.when(jnp.logical_not(overflow))
    def _():
        thr_ge = jnp.where(short, -FLT_MAX, thr)
        write(lambda blk, idx: blk >= thr_ge)

    @pl.when(overflow)
    def _():
        total_bits = max(1, (n_cols - 1).bit_length())

        def ibisect(j, x):
            cand = x | lax.shift_left(jnp.int32(1), total_bits - 1 - j)
            c = count(lambda blk, idx: (blk == thr) & (idx < cand))
            return jnp.where(c < need, cand, x)

        x = lax.fori_loop(0, total_bits, ibisect, jnp.zeros((rows, 1), I32))
        xk = jnp.where(need > 0, x, -1)
        write(lambda blk, idx: (blk > thr) | ((blk == thr) & (idx <= xk)))


def _dsa_prompt_kernel(qh_ref, iqh_ref, iw_ref, ikt_ref, kt_ref, vx_ref, o_ref,
                       sc_ref, bias_ref, wb_ref, *, n_keep):
    i = pl.program_id(1)
    n_chunks = (i * BLOCK + BLOCK + KEY_CHUNK - 1) // KEY_CHUNK
    q_pos = i * BLOCK + lax.broadcasted_iota(I32, (BLOCK, KEY_CHUNK), 0)

    for h in range(IDX_HEADS):
        wb_ref[h] = jnp.broadcast_to(iw_ref[:, h:h + 1], (BLOCK, LANES))
    iq = iqh_ref[...].reshape(IDX_HEADS * BLOCK, IDX_DIM)

    def score_chunk(c, _):
        off = pl.multiple_of(c * KEY_CHUNK, KEY_CHUNK)
        s = jnp.dot(iq, ikt_ref[:, pl.ds(off, KEY_CHUNK)], preferred_element_type=F32)
        parts = []
        for j in range(KEY_CHUNK // LANES):
            acc = None
            for h in range(IDX_HEADS):
                term = wb_ref[h] * jnp.maximum(
                    s[h * BLOCK:(h + 1) * BLOCK, j * LANES:(j + 1) * LANES], 0.0)
                acc = term if acc is None else acc + term
            parts.append(acc)
        score = jnp.concatenate(parts, axis=1)
        k_pos = off + lax.broadcasted_iota(I32, (BLOCK, KEY_CHUNK), 1)
        sc_ref[:, pl.ds(off, KEY_CHUNK)] = jnp.where(k_pos <= q_pos, score, -jnp.inf)
        return 0

    lax.fori_loop(0, n_chunks, score_chunk, 0)
    _select_bias([(sc_ref, bias_ref, n_chunks, KEY_CHUNK, 0)], n_keep, sc_ref.shape[1])

    rows = GROUP * BLOCK
    n_steps = (n_chunks + ATTN_UNROLL - 1) // ATTN_UNROLL

    def mask_tail(c, _):
        bias_ref[:, pl.ds(pl.multiple_of(c * KEY_CHUNK, KEY_CHUNK), KEY_CHUNK)] = jnp.full(
            (BLOCK, KEY_CHUNK), NEG, F32)
        return 0

    lax.fori_loop(n_chunks, n_steps * ATTN_UNROLL, mask_tail, 0)
    for g in range(N_KV_HEADS):
        qg = qh_ref[g * GROUP:(g + 1) * GROUP].reshape(rows, HEAD_DIM)

        def logits(c, g=g, qg=qg):
            off = pl.multiple_of(c * KEY_CHUNK, KEY_CHUNK)
            s = jnp.dot(qg, kt_ref[g * HEAD_DIM:(g + 1) * HEAD_DIM, pl.ds(off, KEY_CHUNK)],
                        preferred_element_type=F32)
            b = bias_ref[:, pl.ds(off, KEY_CHUNK)]
            return (s.reshape(GROUP, BLOCK, KEY_CHUNK) + b[None]).reshape(rows, KEY_CHUNK), off

        def row_max(c2, mx):
            for u in range(ATTN_UNROLL):
                s, _ = logits(c2 * ATTN_UNROLL + u)
                for j in range(KEY_CHUNK // LANES):
                    mx = jnp.maximum(mx, s[:, j * LANES:(j + 1) * LANES])
            return mx

        mx = lax.fori_loop(0, n_steps, row_max, jnp.full((rows, LANES), NEG, F32))
        m = jnp.max(mx, axis=1, keepdims=True)

        def attend(c2, acc, g=g, m=m):
            for u in range(ATTN_UNROLL):
                s, off = logits(c2 * ATTN_UNROLL + u)
                p = jnp.exp2(s - m).astype(BF16)
                acc = acc + jnp.dot(p, vx_ref[g, pl.ds(off, KEY_CHUNK), :],
                                    preferred_element_type=F32)
            return acc

        acc = lax.fori_loop(0, n_steps, attend, jnp.zeros((rows, LANES), F32))
        og = acc[:, :HEAD_DIM] * (1.0 / acc[:, HEAD_DIM:HEAD_DIM + 1])
        for j in range(GROUP // 2):
            pair = jnp.concatenate([og[(2 * j) * BLOCK:(2 * j + 1) * BLOCK],
                                    og[(2 * j + 1) * BLOCK:(2 * j + 2) * BLOCK]], axis=1)
            col = g * GROUP * HEAD_DIM + j * LANES
            o_ref[:, col:col + LANES] = pair


def _dsa_prompt(qh, iqh, iw, ikt, kt, vx, batch, s_len):
    nb = s_len // BLOCK
    n_keep = min(TOPK_MAX, s_len // 4)
    assert s_len % (KEY_CHUNK * ATTN_UNROLL) == 0
    m = batch * s_len
    rows = GROUP * BLOCK
    blk3 = lambda b, i: (0, b * nb + i, 0)
    return pl.pallas_call(
        functools.partial(_dsa_prompt_kernel, n_keep=n_keep),
        grid=(batch, nb),
        in_specs=[pl.BlockSpec((N_HEADS, BLOCK, HEAD_DIM), blk3),
                  pl.BlockSpec((IDX_HEADS, BLOCK, IDX_DIM), blk3),
                  pl.BlockSpec((BLOCK, IDX_HEADS), lambda b, i: (b * nb + i, 0)),
                  pl.BlockSpec((IDX_DIM, s_len), lambda b, i: (0, b)),
                  pl.BlockSpec((KV_WIDTH, s_len), lambda b, i: (0, b)),
                  pl.BlockSpec((N_KV_HEADS, s_len, LANES), lambda b, i: (0, b, 0))],
        out_specs=pl.BlockSpec((BLOCK, BRANCH), lambda b, i: (b * nb + i, 0)),
        out_shape=jax.ShapeDtypeStruct((m, BRANCH), F32),
        scratch_shapes=[pltpu.VMEM((BLOCK, s_len), F32), pltpu.VMEM((BLOCK, s_len), F32),
                        pltpu.VMEM((IDX_HEADS, BLOCK, LANES), F32)],
        compiler_params=_params(2),
        name="dsa_prompt",
    )(qh, iqh, iw, ikt, kt, vx)


def _swa_prompt_kernel(sink_ref, qh_ref, ktp_ref, ktc_ref, vp_ref, vc_ref, o_ref):
    i = pl.program_id(1)
    rows = GROUP * BLOCK
    t = lax.broadcasted_iota(I32, (BLOCK, BLOCK), 0)
    j = lax.broadcasted_iota(I32, (BLOCK, BLOCK), 1)
    d_prev = BLOCK + t - j
    d_cur = t - j
    ok_prev = (d_prev >= 0) & (d_prev <= WINDOW) & (i > 0)
    ok_cur = (d_cur >= 0) & (d_cur <= WINDOW)
    b_prev = jnp.where(ok_prev, 0.0, NEG).astype(F32)
    b_cur = jnp.where(ok_cur, 0.0, NEG).astype(F32)
    for g in range(N_KV_HEADS):
        qg = qh_ref[g * GROUP:(g + 1) * GROUP].reshape(rows, HEAD_DIM)
        rs = slice(g * HEAD_DIM, (g + 1) * HEAD_DIM)
        sp = jnp.dot(qg, ktp_ref[rs, :], preferred_element_type=F32)
        sc = jnp.dot(qg, ktc_ref[rs, :], preferred_element_type=F32)
        sp = (sp.reshape(GROUP, BLOCK, BLOCK) + b_prev[None]).reshape(rows, BLOCK)
        sc = (sc.reshape(GROUP, BLOCK, BLOCK) + b_cur[None]).reshape(rows, BLOCK)
        sink = jnp.concatenate(
            [jnp.full((BLOCK, 1), sink_ref[g * GROUP + r] * LOG2E, F32) for r in range(GROUP)],
            axis=0)
        m = jnp.maximum(jnp.maximum(jnp.max(sp, axis=1, keepdims=True),
                                    jnp.max(sc, axis=1, keepdims=True)), sink)
        pp = jnp.exp2(sp - m).astype(BF16)
        pc = jnp.exp2(sc - m).astype(BF16)
        ox = (jnp.dot(pp, vp_ref[g], preferred_element_type=F32)
              + jnp.dot(pc, vc_ref[g], preferred_element_type=F32))
        denom = ox[:, HEAD_DIM:HEAD_DIM + 1] + jnp.exp2(sink - m)
        og = ox[:, :HEAD_DIM] * (1.0 / denom)
        for jj in range(GROUP // 2):
            pair = jnp.concatenate([og[(2 * jj) * BLOCK:(2 * jj + 1) * BLOCK],
                                    og[(2 * jj + 1) * BLOCK:(2 * jj + 2) * BLOCK]], axis=1)
            col = g * GROUP * HEAD_DIM + jj * LANES
            o_ref[:, col:col + LANES] = pair


def _swa_prompt(sinks_l, qh, kt, vx, batch, s_len):
    assert WINDOW <= BLOCK
    nb = s_len // BLOCK
    m = batch * s_len
    cur = lambda b, i: b * nb + i
    prev = lambda b, i: b * nb + jnp.maximum(i - 1, 0)
    return pl.pallas_call(
        _swa_prompt_kernel,
        grid=(batch, nb),
        in_specs=[pl.BlockSpec(memory_space=pltpu.SMEM),
                  pl.BlockSpec((N_HEADS, BLOCK, HEAD_DIM), lambda b, i: (0, cur(b, i), 0)),
                  pl.BlockSpec((KV_WIDTH, BLOCK), lambda b, i: (0, prev(b, i))),
                  pl.BlockSpec((KV_WIDTH, BLOCK), lambda b, i: (0, cur(b, i))),
                  pl.BlockSpec((N_KV_HEADS, BLOCK, LANES), lambda b, i: (0, prev(b, i), 0)),
                  pl.BlockSpec((N_KV_HEADS, BLOCK, LANES), lambda b, i: (0, cur(b, i), 0))],
        out_specs=pl.BlockSpec((BLOCK, BRANCH), lambda b, i: (cur(b, i), 0)),
        out_shape=jax.ShapeDtypeStruct((m, BRANCH), F32),
        compiler_params=_params(2),
        name="swa_prompt",
    )(sinks_l, qh, kt, kt, vx, vx)


def _block_diag_q(qh, n_seq, t_len):
    q = qh.reshape(N_KV_HEADS, GROUP, n_seq, t_len, HEAD_DIM)
    q = jnp.transpose(q, (2, 3, 0, 1, 4))
    eye = jnp.eye(N_KV_HEADS, dtype=q.dtype)
    qbd = q[:, :, :, :, None, :] * eye[None, None, :, None, :, None]
    return qbd.reshape(n_seq, t_len * N_HEADS, KV_WIDTH)


def _block_diag_out(acc, n_seq, t_len):
    a = acc.reshape(n_seq, t_len, N_KV_HEADS, GROUP, N_KV_HEADS, HEAD_DIM)
    idx = jnp.arange(N_KV_HEADS)
    a = a[:, :, idx, :, idx, :]
    return jnp.transpose(a, (1, 2, 0, 3, 4)).reshape(n_seq * t_len, BRANCH)


def _rows_from_tokens(x, t_len):
    return jnp.concatenate(
        [jnp.broadcast_to(x[t:t + 1], (N_HEADS, x.shape[1])) for t in range(t_len)], axis=0)


def _seqs_per_step(t_len):
    assert SUBLANES % t_len == 0
    return SUBLANES // t_len


def _page_specs(width, la, seqs):
    def spec(s, r):
        return pl.BlockSpec(
            (1, 1, width, PAGE_SIZE),
            lambda b, j, pt: (la, pt[b * seqs + s, j * PAGES_PER_STEP + r], 0, 0))
    return [spec(s, r) for s in range(seqs) for r in range(PAGES_PER_STEP)]


def _pages(refs, s):
    return jnp.concatenate([refs[s * PAGES_PER_STEP + r][0, 0] for r in range(PAGES_PER_STEP)],
                           axis=1).astype(BF16)


def _dsa_sample_score_kernel(pt_ref, iq_ref, w_ref, ikn_ref, *rest, t_len, seqs):
    ik_refs = rest[:seqs * PAGES_PER_STEP]
    sc_ref, scn_ref = rest[seqs * PAGES_PER_STEP:]
    j = pl.program_id(1)

    def scores(s, keys_t):
        x = jnp.dot(iq_ref[s], keys_t, preferred_element_type=F32)
        x = w_ref[s] * jnp.maximum(x, 0.0)
        return jnp.sum(x.reshape(t_len, IDX_HEADS, x.shape[1]), axis=1)

    sc_ref[...] = jnp.concatenate([scores(s, _pages(ik_refs, s)) for s in range(seqs)], axis=0)

    @pl.when(j == pl.num_programs(1) - 1)
    def _():
        tq = lax.broadcasted_iota(I32, (t_len, LANES), 0)
        kn = lax.broadcasted_iota(I32, (t_len, LANES), 1)
        scn_ref[...] = jnp.concatenate(
            [jnp.where(kn <= tq, scores(s, ikn_ref[s].astype(BF16)), -jnp.inf)
             for s in range(seqs)], axis=0)


def _dsa_sample_score(page_table, iq_rows, w_rows, ik_new_t, cache_ik_t, la, t_len):
    n_seq, n_pages = page_table.shape
    past = n_pages * PAGE_SIZE
    seqs = _seqs_per_step(t_len)
    assert n_pages % PAGES_PER_STEP == 0 and n_seq % seqs == 0
    n_steps = n_pages // PAGES_PER_STEP
    chunk = PAGES_PER_STEP * PAGE_SIZE
    seq3 = lambda b, j, pt: (b, 0, 0)
    grid_spec = pltpu.PrefetchScalarGridSpec(
        num_scalar_prefetch=1,
        grid=(n_seq // seqs, n_steps),
        in_specs=[pl.BlockSpec((seqs, t_len * IDX_HEADS, IDX_DIM), seq3),
                  pl.BlockSpec((seqs, t_len * IDX_HEADS, 1), seq3),
                  pl.BlockSpec((seqs, IDX_DIM, LANES), seq3)]
                 + _page_specs(IDX_DIM, la, seqs),
        out_specs=[pl.BlockSpec((SUBLANES, chunk), lambda b, j, pt: (b, j)),
                   pl.BlockSpec((SUBLANES, LANES), lambda b, j, pt: (b, 0))],
    )
    return pl.pallas_call(
        functools.partial(_dsa_sample_score_kernel, t_len=t_len, seqs=seqs),
        grid_spec=grid_spec,
        out_shape=[jax.ShapeDtypeStruct((n_seq * t_len, past), F32),
                   jax.ShapeDtypeStruct((n_seq * t_len, LANES), F32)],
        compiler_params=_params(2),
        name="dsa_sample_score",
    )(page_table, iq_rows, w_rows, ik_new_t, *([cache_ik_t] * (seqs * PAGES_PER_STEP)))


def _dsa_sample_select_kernel(sc_ref, scn_ref, bias_ref, biasn_ref, *, n_keep):
    past = sc_ref.shape[1]
    _select_bias([(sc_ref, bias_ref, past // KEY_CHUNK, KEY_CHUNK, 0),
                  (scn_ref, biasn_ref, 1, LANES, past)], n_keep, past + LANES)


def _dsa_sample_select(scores, scores_new, n_keep):
    n_rows, past = scores.shape
    rows = 128 if n_rows % 128 == 0 else 8
    assert n_rows % rows == 0 and past % KEY_CHUNK == 0
    row = lambda i: (i, 0)
    return pl.pallas_call(
        functools.partial(_dsa_sample_select_kernel, n_keep=n_keep),
        grid=(n_rows // rows,),
        in_specs=[pl.BlockSpec((rows, past), row), pl.BlockSpec((rows, LANES), row)],
        out_specs=[pl.BlockSpec((rows, past), row), pl.BlockSpec((rows, LANES), row)],
        out_shape=[jax.ShapeDtypeStruct((n_rows, past), F32),
                   jax.ShapeDtypeStruct((n_rows, LANES), F32)],
        compiler_params=_params(1),
        name="dsa_sample_select",
    )(scores, scores_new)


def _dsa_sample_attend_kernel(pt_ref, q_ref, bias_ref, biasn_ref, kn_ref, vn_ref, *rest,
                              t_len, seqs):
    n_pages = seqs * PAGES_PER_STEP
    k_refs = rest[:n_pages]
    v_refs = rest[n_pages:2 * n_pages]
    o_ref, m_ref, l_ref, acc_ref = rest[2 * n_pages:]
    j = pl.program_id(1)
    n_steps = pl.num_programs(1)

    @pl.when(j == 0)
    def _():
        m_ref[...] = jnp.full(m_ref.shape, NEG, F32)
        l_ref[...] = jnp.zeros(l_ref.shape, F32)
        acc_ref[...] = jnp.zeros(acc_ref.shape, F32)

    def update(s_i, k_t, v_t, bias_tok):
        q = q_ref[s_i]
        s = jnp.dot(q, k_t, preferred_element_type=F32) + _rows_from_tokens(bias_tok, t_len)
        m = m_ref[s_i]
        m_new = jnp.maximum(m, jnp.max(s, axis=1, keepdims=True))
        alpha = jnp.exp2(m - m_new)
        p = jnp.exp2(s - m_new)
        l_ref[s_i] = alpha * l_ref[s_i] + jnp.sum(p, axis=1, keepdims=True)
        acc_ref[s_i] = alpha * acc_ref[s_i] + lax.dot_general(
            p.astype(BF16), v_t, NT_DIMS, preferred_element_type=F32)
        m_ref[s_i] = m_new

    for s_i in range(seqs):
        tok = slice(s_i * t_len, (s_i + 1) * t_len)
        update(s_i, _pages(k_refs, s_i), _pages(v_refs, s_i), bias_ref[tok, :])

    @pl.when(j == n_steps - 1)
    def _():
        for s_i in range(seqs):
            tok = slice(s_i * t_len, (s_i + 1) * t_len)
            update(s_i, kn_ref[s_i].astype(BF16), vn_ref[s_i].astype(BF16), biasn_ref[tok, :])
            o_ref[s_i] = acc_ref[s_i] / l_ref[s_i]


def _dsa_sample_attend(page_table, q_bd, bias, bias_new, k_new_t, v_new_t, cache_k_t, cache_v_t,
                       la, t_len):
    n_seq, n_pages = page_table.shape
    n_steps = n_pages // PAGES_PER_STEP
    seqs = _seqs_per_step(t_len)
    rows = t_len * N_HEADS
    chunk = PAGES_PER_STEP * PAGE_SIZE
    seq3 = lambda b, j, pt: (b, 0, 0)
    grid_spec = pltpu.PrefetchScalarGridSpec(
        num_scalar_prefetch=1,
        grid=(n_seq // seqs, n_steps),
        in_specs=[pl.BlockSpec((seqs, rows, KV_WIDTH), seq3),
                  pl.BlockSpec((SUBLANES, chunk), lambda b, j, pt: (b, j)),
                  pl.BlockSpec((SUBLANES, LANES), lambda b, j, pt: (b, 0)),
                  pl.BlockSpec((seqs, KV_WIDTH, LANES), seq3),
                  pl.BlockSpec((seqs, KV_WIDTH, LANES), seq3)]
                 + _page_specs(KV_WIDTH, la, seqs) * 2,
        out_specs=pl.BlockSpec((seqs, rows, KV_WIDTH), seq3),
        scratch_shapes=[pltpu.VMEM((seqs, rows, 1), F32), pltpu.VMEM((seqs, rows, 1), F32),
                        pltpu.VMEM((seqs, rows, KV_WIDTH), F32)],
    )
    return pl.pallas_call(
        functools.partial(_dsa_sample_attend_kernel, t_len=t_len, seqs=seqs),
        grid_spec=grid_spec,
        out_shape=jax.ShapeDtypeStruct((n_seq, rows, KV_WIDTH), F32),
        compiler_params=_params(2),
        name="dsa_sample_attend",
    )(page_table, q_bd, bias, bias_new, k_new_t, v_new_t,
      *([cache_k_t] * (seqs * PAGES_PER_STEP)), *([cache_v_t] * (seqs * PAGES_PER_STEP)))


def _swa_sample_kernel(sink_ref, q_ref, kn_ref, vn_ref, wk_ref, wv_ref, o_ref, nwk_ref, nwv_ref,
                       *, t_len, seqs):
    wb = wk_ref.shape[2]
    rows = t_len * N_HEADS
    t_row = lax.broadcasted_iota(I32, (rows, wb), 0) // N_HEADS
    i_col = lax.broadcasted_iota(I32, (rows, wb), 1)
    d_state = wb + t_row - i_col
    b_state = jnp.where((d_state >= 0) & (d_state <= WINDOW), 0.0, NEG).astype(F32)
    t_row_n = lax.broadcasted_iota(I32, (rows, LANES), 0) // N_HEADS
    j_col = lax.broadcasted_iota(I32, (rows, LANES), 1)
    d_new = t_row_n - j_col
    b_new = jnp.where((d_new >= 0) & (d_new <= WINDOW) & (j_col < t_len), 0.0, NEG).astype(F32)
    h_row = lax.broadcasted_iota(I32, (rows, 1), 0) % N_HEADS
    sink = jnp.zeros((rows, 1), F32)
    for h in range(N_HEADS):
        sink = jnp.where(h_row == h, sink_ref[h] * LOG2E, sink)
    keep_old = lax.broadcasted_iota(I32, (KV_WIDTH, wb), 1) < wb - t_len
    for s_i in range(seqs):
        q = q_ref[s_i]
        wk = wk_ref[s_i]
        wv = wv_ref[s_i]
        kn = kn_ref[s_i]
        vn = vn_ref[s_i]
        ss = jnp.dot(q, wk.astype(BF16), preferred_element_type=F32) + b_state
        sn = jnp.dot(q, kn.astype(BF16), preferred_element_type=F32) + b_new
        m = jnp.maximum(jnp.maximum(jnp.max(ss, axis=1, keepdims=True),
                                    jnp.max(sn, axis=1, keepdims=True)), sink)
        ps = jnp.exp2(ss - m)
        pn = jnp.exp2(sn - m)
        denom = (jnp.sum(ps, axis=1, keepdims=True) + jnp.sum(pn, axis=1, keepdims=True)
                 + jnp.exp2(sink - m))
        inv = 1.0 / denom
        o_ref[s_i] = (lax.dot_general((ps * inv).astype(BF16), wv.astype(BF16), NT_DIMS,
                                      preferred_element_type=F32)
                      + lax.dot_general((pn * inv).astype(BF16), vn.astype(BF16), NT_DIMS,
                                        preferred_element_type=F32))
        nwk_ref[s_i] = jnp.where(keep_old, pltpu.roll(wk, wb - t_len, 1), pltpu.roll(kn, wb - t_len, 1))
        nwv_ref[s_i] = jnp.where(keep_old, pltpu.roll(wv, wb - t_len, 1), pltpu.roll(vn, wb - t_len, 1))


def _swa_sample(sinks_l, q_bd, k_new_t, v_new_t, state_wk_t, state_wv_t, lb, t_len):
    _, n_seq, _, wb = state_wk_t.shape
    seqs = 8 if n_seq % 8 == 0 else 1
    rows = t_len * N_HEADS
    assert wb == LANES and wb >= t_len
    seq3 = lambda b: (b, 0, 0)
    st4 = lambda b: (lb, b, 0, 0)
    return pl.pallas_call(
        functools.partial(_swa_sample_kernel, t_len=t_len, seqs=seqs),
        grid=(n_seq // seqs,),
        in_specs=[pl.BlockSpec(memory_space=pltpu.SMEM),
                  pl.BlockSpec((seqs, rows, KV_WIDTH), seq3),
                  pl.BlockSpec((seqs, KV_WIDTH, LANES), seq3),
                  pl.BlockSpec((seqs, KV_WIDTH, LANES), seq3),
                  pl.BlockSpec((None, seqs, KV_WIDTH, wb), st4),
                  pl.BlockSpec((None, seqs, KV_WIDTH, wb), st4)],
        out_specs=[pl.BlockSpec((seqs, rows, KV_WIDTH), seq3),
                   pl.BlockSpec((seqs, KV_WIDTH, wb), seq3),
                   pl.BlockSpec((seqs, KV_WIDTH, wb), seq3)],
        out_shape=[jax.ShapeDtypeStruct((n_seq, rows, KV_WIDTH), F32),
                   jax.ShapeDtypeStruct((n_seq, KV_WIDTH, wb), F32),
                   jax.ShapeDtypeStruct((n_seq, KV_WIDTH, wb), F32)],
        compiler_params=_params(1),
        name="swa_sample",
    )(sinks_l, q_bd, k_new_t, v_new_t, state_wk_t, state_wv_t)


def _new_rows_t(x, n_seq, t_len):
    xt = jnp.transpose(x.reshape(n_seq, t_len, x.shape[1]), (0, 2, 1))
    return jnp.pad(xt, ((0, 0), (0, 0), (0, LANES - t_len)))


def kernel(x_prompt, x_sample, cache_k, cache_v, cache_ik, page_table, state_wk, state_wv,
           norm_g, final_g, w_in_a, w_out_a, w_in_b, w_out_b, sinks):
    batch, s_len, d = x_prompt.shape
    n_seq, t_len, _ = x_sample.shape
    depth = norm_g.shape[0]
    n_pool = cache_k.shape[1]
    past = page_table.shape[1] * PAGE_SIZE
    wb = state_wk.shape[2]
    wb_p = min(WINDOW, s_len)

    cos_p, sin_p = _rope_tables(jnp.tile(jnp.arange(s_len, dtype=I32), batch))
    cos_s, sin_s = _rope_tables(jnp.tile(past + jnp.arange(t_len, dtype=I32), n_seq))
    ck_t = jnp.transpose(cache_k, (0, 1, 3, 4, 2)).reshape(cache_k.shape[0], n_pool, KV_WIDTH, PAGE_SIZE)
    cv_t = jnp.transpose(cache_v, (0, 1, 3, 4, 2)).reshape(cache_v.shape[0], n_pool, KV_WIDTH, PAGE_SIZE)
    cik_t = jnp.transpose(cache_ik, (0, 1, 3, 2))
    swk_t = jnp.transpose(state_wk, (0, 1, 3, 4, 2)).reshape(state_wk.shape[0], n_seq, KV_WIDTH, wb)
    swv_t = jnp.transpose(state_wv, (0, 1, 3, 4, 2)).reshape(state_wv.shape[0], n_seq, KV_WIDTH, wb)
    fg = final_g.reshape(1, d)

    hp = x_prompt.reshape(batch * s_len, d)
    hs = x_sample.reshape(n_seq * t_len, d)
    kp_l, vp_l, ikp_l, ks_l, vs_l, iks_l = [], [], [], [], [], []
    wkp_l, wvp_l, wks_l, wvs_l = [], [], [], []
    for i in range(depth):
        g = norm_g[i].reshape(1, d)
        final = i == depth - 1
        if i % N_MIXERS == 0:
            la = i // N_MIXERS
            w_in = jnp.pad(w_in_a[la], ((0, 0), (0, A_PAD_COLS - w_in_a.shape[2]))).astype(BF16)
            w_out = w_out_a[la].astype(BF16)
            qh, k32, v32, kt, vh, gate, iqh, ik32, ikt, iw = _project(hp, g, w_in, cos_p, sin_p, "a")
            o = _dsa_prompt(qh, iqh, iw, ikt, kt, vh, batch, s_len)
            hp = _gated_out(o, gate, w_out, hp, fg, final)
            kp_l.append(k32.reshape(batch, s_len, N_KV_HEADS, HEAD_DIM))
            vp_l.append(v32.reshape(batch, s_len, N_KV_HEADS, HEAD_DIM))
            ikp_l.append(ik32.reshape(batch, s_len, IDX_DIM))
            qh, k32, v32, kt, vh, gate, iqh, ik32, ikt, iw = _project(hs, g, w_in, cos_s, sin_s, "a")
            iq_rows = jnp.transpose(iqh.reshape(IDX_HEADS, n_seq, t_len, IDX_DIM),
                                    (1, 2, 0, 3)).reshape(n_seq, t_len * IDX_HEADS, IDX_DIM)
            w_rows = iw.reshape(n_seq, t_len * IDX_HEADS, 1)
            sc, sc_new = _dsa_sample_score(page_table, iq_rows, w_rows,
                                           _new_rows_t(ik32, n_seq, t_len), cik_t, la, t_len)
            bias, bias_new = _dsa_sample_select(sc, sc_new, min(TOPK_MAX, (past + t_len) // 4))
            acc = _dsa_sample_attend(page_table, _block_diag_q(qh, n_seq, t_len), bias, bias_new,
                                     _new_rows_t(k32, n_seq, t_len), _new_rows_t(v32, n_seq, t_len),
                                     ck_t, cv_t, la, t_len)
            hs = _gated_out(_block_diag_out(acc, n_seq, t_len), gate, w_out, hs, fg, final)
            ks_l.append(k32.reshape(n_seq, t_len, N_KV_HEADS, HEAD_DIM))
            vs_l.append(v32.reshape(n_seq, t_len, N_KV_HEADS, HEAD_DIM))
            iks_l.append(ik32.reshape(n_seq, t_len, IDX_DIM))
        else:
            lb = i // N_MIXERS
            w_in = w_in_b[lb].astype(BF16)
            w_out = w_out_b[lb].astype(BF16)
            qh, k32, v32, kt, vh, gate = _project(hp, g, w_in, cos_p, sin_p, "b")
            o = _swa_prompt(sinks[lb], qh, kt, vh, batch, s_len)
            hp = _gated_out(o, gate, w_out, hp, fg, final)
            k4 = k32.reshape(batch, s_len, N_KV_HEADS, HEAD_DIM)
            v4 = v32.reshape(batch, s_len, N_KV_HEADS, HEAD_DIM)
            wkp_l.append(k4[:, s_len - wb_p:])
            wvp_l.append(v4[:, s_len - wb_p:])
            qh, k32, v32, kt, vh, gate = _project(hs, g, w_in, cos_s, sin_s, "b")
            acc, nwk, nwv = _swa_sample(sinks[lb], _block_diag_q(qh, n_seq, t_len),
                                        _new_rows_t(k32, n_seq, t_len), _new_rows_t(v32, n_seq, t_len),
                                        swk_t, swv_t, lb, t_len)
            hs = _gated_out(_block_diag_out(acc, n_seq, t_len), gate, w_out, hs, fg, final)
            back = lambda a: jnp.transpose(a.reshape(n_seq, N_KV_HEADS, HEAD_DIM, wb), (0, 3, 1, 2))
            wks_l.append(back(nwk))
            wvs_l.append(back(nwv))

    assert depth >= 1
    return (hp.reshape(batch, s_len, d), hs.reshape(n_seq, t_len, d),
            jnp.stack(kp_l), jnp.stack(vp_l), jnp.stack(ikp_l),
            jnp.stack(ks_l), jnp.stack(vs_l), jnp.stack(iks_l),
            jnp.stack(wkp_l), jnp.stack(wvp_l), jnp.stack(wks_l), jnp.stack(wvs_l))
```

```python
import functools

import jax
import jax.numpy as jnp
from jax import lax
from jax.experimental import pallas as pl
from jax.experimental.pallas import tpu as pltpu

HEAD_DIM = 64
N_HEADS = 16
N_KV_HEADS = 4
GROUP = N_HEADS // N_KV_HEADS
BRANCH = N_HEADS * HEAD_DIM
KV_WIDTH = N_KV_HEADS * HEAD_DIM
IDX_HEADS = 8
IDX_DIM = 64
TOPK_MAX = 256
WINDOW = 128
BLOCK = 128
PAGE_SIZE = 128
ROPE_THETA = 10000.0
RMS_EPS = 1e-6
N_MIXERS = 2

LANES = 128
SUBLANES = 8
KEY_CHUNK = 512
PAGES_PER_STEP = 16
NEG = -1e30
LOG2E = 1.4426950408889634
Q_SCALE = HEAD_DIM ** -0.5 * LOG2E
INT_MIN = -(2 ** 31)
FLT_MAX = 3.4028234663852886e38
VMEM_LIMIT = 56 * 1024 * 1024

A_PAD_COLS = 3200
OFF_Q, OFF_K, OFF_V, OFF_Z, OFF_IQ, OFF_IK, OFF_IW = 0, 1024, 1280, 1536, 2560, 3072, 3136

F32 = jnp.float32
BF16 = jnp.bfloat16
I32 = jnp.int32
NT_DIMS = (((1,), (1,)), ((), ()))


def _params(n_axes):
    return pltpu.CompilerParams(dimension_semantics=("arbitrary",) * n_axes,
                                vmem_limit_bytes=VMEM_LIMIT)


def _rope_tables(pos):
    half = HEAD_DIM // 2
    inv = ROPE_THETA ** (-jnp.arange(half, dtype=F32) * 2.0 / HEAD_DIM)
    ang = pos.astype(F32)[:, None] * inv[None, :]
    cos, sin = jnp.cos(ang), jnp.sin(ang)
    return (jnp.concatenate([cos, cos, cos, cos], axis=1),
            jnp.concatenate([-sin, sin, -sin, sin], axis=1))


def _project_kernel(h_ref, g_ref, w_ref, cos_ref, sin_ref, *out_refs, kind):
    if kind == "a":
        (qh_ref, k32_ref, v32_ref, kt_ref, vx_ref, gate_ref,
         iqh_ref, ik32_ref, ikt_ref, iw_ref) = out_refs
    else:
        qh_ref, k32_ref, v32_ref, kt_ref, vx_ref, gate_ref = out_refs
    tm = h_ref.shape[0]
    x = h_ref[...]
    var = jnp.mean(x * x, axis=-1, keepdims=True)
    y = (x * lax.rsqrt(var + RMS_EPS) * g_ref[...]).astype(BF16)
    cos = cos_ref[...]
    sin = sin_ref[...]
    lane = lax.broadcasted_iota(I32, (tm, LANES), 1)
    is_lo = (lane & (HEAD_DIM // 2)) == 0

    def rope(u):
        partner = jnp.where(is_lo, pltpu.roll(u, LANES - HEAD_DIM // 2, 1),
                            pltpu.roll(u, HEAD_DIM // 2, 1))
        return u * cos + partner * sin

    def proj(off, width):
        return jnp.dot(y, w_ref[:, off:off + width], preferred_element_type=F32)

    scale = Q_SCALE
    ones_col = jnp.where(lane == HEAD_DIM, 1.0, 0.0).astype(F32)
    uq = proj(OFF_Q, BRANCH)
    for c in range(BRANCH // LANES):
        r = (rope(uq[:, c * LANES:(c + 1) * LANES]) * scale).astype(BF16)
        qh_ref[2 * c] = r[:, :HEAD_DIM]
        qh_ref[2 * c + 1] = r[:, HEAD_DIM:]
    uk = proj(OFF_K, KV_WIDTH)
    uv = proj(OFF_V, KV_WIDTH)
    for c in range(KV_WIDTH // LANES):
        r = rope(uk[:, c * LANES:(c + 1) * LANES])
        k32_ref[:, c * LANES:(c + 1) * LANES] = r
        kt_ref[c * LANES:(c + 1) * LANES, :] = r.T.astype(BF16)
        vv = uv[:, c * LANES:(c + 1) * LANES]
        v32_ref[:, c * LANES:(c + 1) * LANES] = vv
        vx_ref[2 * c] = jnp.where(lane < HEAD_DIM, vv, ones_col).astype(BF16)
        vx_ref[2 * c + 1] = jnp.where(lane < HEAD_DIM, pltpu.roll(vv, HEAD_DIM, 1), ones_col).astype(BF16)
    uz = proj(OFF_Z, BRANCH)
    gate_ref[...] = uz * (1.0 / (1.0 + jnp.exp(-uz)))
    if kind == "a":
        iscale = IDX_DIM ** -0.5
        ui = proj(OFF_IQ, IDX_HEADS * IDX_DIM)
        for c in range(IDX_HEADS * IDX_DIM // LANES):
            r = (rope(ui[:, c * LANES:(c + 1) * LANES]) * iscale).astype(BF16)
            iqh_ref[2 * c] = r[:, :IDX_DIM]
            iqh_ref[2 * c + 1] = r[:, IDX_DIM:]
        ut = proj(OFF_IK, LANES)
        r = rope(ut)
        ik32_ref[...] = r[:, :IDX_DIM]
        ikt_ref[...] = r.T[:IDX_DIM, :].astype(BF16)
        iw_ref[...] = ut[:, OFF_IW - OFF_IK:OFF_IW - OFF_IK + IDX_HEADS] * (IDX_HEADS ** -0.5)


def _project(h, g, w, cos_t, sin_t, kind):
    m, d = h.shape
    tm = 512 if m % 512 == 0 and m > 512 else 256
    assert m % tm == 0
    wcols = w.shape[1]
    row = lambda i: (i, 0)
    const = lambda i: (0, 0)
    out_shape = [
        jax.ShapeDtypeStruct((N_HEADS, m, HEAD_DIM), BF16),
        jax.ShapeDtypeStruct((m, KV_WIDTH), F32),
        jax.ShapeDtypeStruct((m, KV_WIDTH), F32),
        jax.ShapeDtypeStruct((KV_WIDTH, m), BF16),
        jax.ShapeDtypeStruct((N_KV_HEADS, m, LANES), BF16),
        jax.ShapeDtypeStruct((m, BRANCH), F32),
    ]
    out_specs = [
        pl.BlockSpec((N_HEADS, tm, HEAD_DIM), lambda i: (0, i, 0)),
        pl.BlockSpec((tm, KV_WIDTH), row),
        pl.BlockSpec((tm, KV_WIDTH), row),
        pl.BlockSpec((KV_WIDTH, tm), lambda i: (0, i)),
        pl.BlockSpec((N_KV_HEADS, tm, LANES), lambda i: (0, i, 0)),
        pl.BlockSpec((tm, BRANCH), row),
    ]
    if kind == "a":
        out_shape += [
            jax.ShapeDtypeStruct((IDX_HEADS, m, IDX_DIM), BF16),
            jax.ShapeDtypeStruct((m, IDX_DIM), F32),
            jax.ShapeDtypeStruct((IDX_DIM, m), BF16),
            jax.ShapeDtypeStruct((m, IDX_HEADS), F32),
        ]
        out_specs += [
            pl.BlockSpec((IDX_HEADS, tm, IDX_DIM), lambda i: (0, i, 0)),
            pl.BlockSpec((tm, IDX_DIM), row),
            pl.BlockSpec((IDX_DIM, tm), lambda i: (0, i)),
            pl.BlockSpec((tm, IDX_HEADS), row),
        ]
    return pl.pallas_call(
        functools.partial(_project_kernel, kind=kind),
        grid=(m // tm,),
        in_specs=[pl.BlockSpec((tm, d), row), pl.BlockSpec((1, d), const),
                  pl.BlockSpec((d, wcols), const),
                  pl.BlockSpec((tm, LANES), row), pl.BlockSpec((tm, LANES), row)],
        out_specs=out_specs,
        out_shape=out_shape,
        compiler_params=_params(1),
        name=f"project_{kind}",
    )(h, g, w, cos_t, sin_t)


def _out_kernel(o_ref, gate_ref, w_ref, h_ref, fg_ref, y_ref, *, final):
    x = (o_ref[...] * gate_ref[...]).astype(BF16)
    y = h_ref[...] + jnp.dot(x, w_ref[...], preferred_element_type=F32)
    if final:
        var = jnp.mean(y * y, axis=-1, keepdims=True)
        y = y * lax.rsqrt(var + RMS_EPS) * fg_ref[...]
    y_ref[...] = y


def _gated_out(o, gate, w, h, fg, final):
    m, d = h.shape
    tm = 512 if m % 512 == 0 and m > 512 else 256
    row = lambda i: (i, 0)
    const = lambda i: (0, 0)
    return pl.pallas_call(
        functools.partial(_out_kernel, final=final),
        grid=(m // tm,),
        in_specs=[pl.BlockSpec((tm, BRANCH), row), pl.BlockSpec((tm, BRANCH), row),
                  pl.BlockSpec((BRANCH, d), const), pl.BlockSpec((tm, d), row),
                  pl.BlockSpec((1, d), const)],
        out_specs=pl.BlockSpec((tm, d), row),
        out_shape=jax.ShapeDtypeStruct((m, d), F32),
        compiler_params=_params(1),
        name="gated_out_final" if final else "gated_out",
    )(o, gate, w, h, fg)


def _key_to_float(key):
    bits = key ^ ((key >> 31) & 0x7FFFFFFF)
    return lax.bitcast_convert_type(bits, F32)


def _select_bias(segments, n_keep, n_cols):
    rows = segments[0][0].shape[0]

    def fold(x):
        acc = x[:, 0:LANES]
        for j in range(1, x.shape[1] // LANES):
            acc = acc + x[:, j * LANES:(j + 1) * LANES]
        return acc

    def for_chunks(fn, init):
        carry = init
        for seg in segments:
            sc_ref, _, n_chunks, chunk, first_col = seg

            def body(c, carry, seg=seg, sc_ref=sc_ref, chunk=chunk, first_col=first_col):
                off = pl.multiple_of(c * chunk, chunk)
                blk = sc_ref[:, pl.ds(off, chunk)]
                idx = first_col + off + lax.broadcasted_iota(I32, (rows, chunk), 1)
                return fn(seg, off, blk, idx, carry)
            carry = lax.fori_loop(0, n_chunks, body, carry)
        return carry

    def count(pred):
        acc = for_chunks(
            lambda seg, off, blk, idx, acc: acc + fold(jnp.where(pred(blk, idx), 1, 0).astype(I32)),
            jnp.zeros((rows, LANES), I32))
        return jnp.sum(acc, axis=1, keepdims=True)

    def bisect(carry):
        j, key, cge = carry
        cand = key ^ lax.shift_left(jnp.int32(1), 31 - j)
        cf = _key_to_float(cand)
        c = count(lambda blk, idx: blk >= cf)
        ok = c >= n_keep
        return j + 1, jnp.where(ok, cand, key), jnp.where(ok, c, cge)

    def unsettled(carry):
        j, _, cge = carry
        return (j < 32) & (jnp.max(jnp.where(cge == n_keep, 0, 1)) > 0)

    _, key, cge = lax.while_loop(
        unsettled, bisect,
        (jnp.int32(0), jnp.full((rows, 1), INT_MIN, I32), jnp.zeros((rows, 1), I32)))
    short = key == INT_MIN
    thr = jnp.where(short, -jnp.inf, _key_to_float(key))
    cgt = count(lambda blk, idx: blk > thr)
    need = jnp.where(short, 0, n_keep - cgt)
    ties = jnp.where(short, 0, cge - cgt)
    overflow = jnp.max(jnp.where(ties > need, 1, 0)) > 0

    def write(keep_fn):
        def store(seg, off, blk, idx, carry):
            seg[1][:, pl.ds(off, seg[3])] = jnp.where(keep_fn(blk, idx), 0.0, NEG).astype(F32)
            return carry
        for_chunks(store, 0)

    @pl.when(jnp.logical_not(overflow))
    def _():
        thr_ge = jnp.where(short, -FLT_MAX, thr)
        write(lambda blk, idx: blk >= thr_ge)

    @pl.when(overflow)
    def _():
        total_bits = max(1, (n_cols - 1).bit_length())

        def ibisect(j, x):
            cand = x | lax.shift_left(jnp.int32(1), total_bits - 1 - j)
            c = count(lambda blk, idx: (blk == thr) & (idx < cand))
            return jnp.where(c < need, cand, x)

        x = lax.fori_loop(0, total_bits, ibisect, jnp.zeros((rows, 1), I32))
        xk = jnp.where(need > 0, x, -1)
        write(lambda blk, idx: (blk > thr) | ((blk == thr) & (idx <= xk)))


def _dsa_prompt_kernel(qh_ref, iqh_ref, iw_ref, ikt_ref, kt_ref, vx_ref, o_ref,
                       sc_ref, bias_ref, wb_ref, s_ref, p_ref, *, n_keep):
    i = pl.program_id(1)
    n_chunks = (i * BLOCK + BLOCK + KEY_CHUNK - 1) // KEY_CHUNK
    q_pos = i * BLOCK + lax.broadcasted_iota(I32, (BLOCK, KEY_CHUNK), 0)

    for g in range(N_KV_HEADS):
        p_ref[g, 1] = jnp.zeros(p_ref.shape[2:], BF16)
    for h in range(IDX_HEADS):
        wb_ref[h] = jnp.broadcast_to(iw_ref[:, h:h + 1], (BLOCK, LANES))
    iq = iqh_ref[...].reshape(IDX_HEADS * BLOCK, IDX_DIM)

    def score_chunk(c, _):
        off = pl.multiple_of(c * KEY_CHUNK, KEY_CHUNK)
        s = jnp.dot(iq, ikt_ref[:, pl.ds(off, KEY_CHUNK)], preferred_element_type=F32)
        parts = []
        for j in range(KEY_CHUNK // LANES):
            acc = None
            for h in range(IDX_HEADS):
                term = wb_ref[h] * jnp.maximum(
                    s[h * BLOCK:(h + 1) * BLOCK, j * LANES:(j + 1) * LANES], 0.0)
                acc = term if acc is None else acc + term
            parts.append(acc)
        score = jnp.concatenate(parts, axis=1)
        k_pos = off + lax.broadcasted_iota(I32, (BLOCK, KEY_CHUNK), 1)
        sc_ref[:, pl.ds(off, KEY_CHUNK)] = jnp.where(k_pos <= q_pos, score, -jnp.inf)
        return 0

    lax.fori_loop(0, n_chunks, score_chunk, 0)
    _select_bias([(sc_ref, bias_ref, n_chunks, KEY_CHUNK, 0)], n_keep, sc_ref.shape[1])

    rows = GROUP * BLOCK
    n_pairs = (n_chunks + 1) // 2
    last = 2 * n_pairs - 1

    @pl.when(last >= n_chunks)
    def _():
        bias_ref[:, pl.ds(pl.multiple_of(last * KEY_CHUNK, KEY_CHUNK), KEY_CHUNK)] = jnp.full(
            (BLOCK, KEY_CHUNK), NEG, F32)

    for g in range(N_KV_HEADS):
        qg = qh_ref[g * GROUP:(g + 1) * GROUP].reshape(rows, HEAD_DIM)

        def pv(c, slot, g=g):
            off = pl.multiple_of(c * KEY_CHUNK, KEY_CHUNK)
            return jnp.dot(p_ref[g, slot], vx_ref[g, pl.ds(off, KEY_CHUNK), :],
                           preferred_element_type=F32)

        def logits(c, g=g, qg=qg):
            off = pl.multiple_of(c * KEY_CHUNK, KEY_CHUNK)
            s = jnp.dot(qg, kt_ref[g * HEAD_DIM:(g + 1) * HEAD_DIM, pl.ds(off, KEY_CHUNK)],
                        preferred_element_type=F32)
            b = bias_ref[:, pl.ds(off, KEY_CHUNK)]
            return (s.reshape(GROUP, BLOCK, KEY_CHUNK) + b[None]).reshape(rows, KEY_CHUNK)

        def trip(c, cur, carry, g=g):
            m, acc = carry
            other = 1 - cur
            acc = acc + pv(jnp.maximum(c - 1, 0), other)
            s = s_ref[g, cur]
            m_new = jnp.maximum(m, jnp.max(s, axis=1, keepdims=True))
            p_ref[g, cur] = jnp.exp2(s - m_new).astype(BF16)
            s_ref[g, other] = logits(jnp.minimum(c + 1, last))
            return m_new, acc * jnp.exp2(m - m_new)

        def trip_pair(k, carry):
            return trip(2 * k + 1, 1, trip(2 * k, 0, carry))

        s_ref[g, 0] = logits(0)
        _, acc = lax.fori_loop(0, n_pairs, trip_pair,
                               (jnp.full((rows, 1), NEG, F32), jnp.zeros((rows, LANES), F32)))
        acc = acc + pv(last, 1)
        og = acc[:, :HEAD_DIM] * (1.0 / acc[:, HEAD_DIM:HEAD_DIM + 1])
        for j in range(GROUP // 2):
            pair = jnp.concatenate([og[(2 * j) * BLOCK:(2 * j + 1) * BLOCK],
                                    og[(2 * j + 1) * BLOCK:(2 * j + 2) * BLOCK]], axis=1)
            col = g * GROUP * HEAD_DIM + j * LANES
            o_ref[:, col:col + LANES] = pair


def _dsa_prompt(qh, iqh, iw, ikt, kt, vx, batch, s_len):
    nb = s_len // BLOCK
    n_keep = min(TOPK_MAX, s_len // 4)
    assert s_len % (2 * KEY_CHUNK) == 0
    m = batch * s_len
    rows = GROUP * BLOCK
    blk3 = lambda b, i: (0, b * nb + i, 0)
    return pl.pallas_call(
        functools.partial(_dsa_prompt_kernel, n_keep=n_keep),
        grid=(batch, nb),
        in_specs=[pl.BlockSpec((N_HEADS, BLOCK, HEAD_DIM), blk3),
                  pl.BlockSpec((IDX_HEADS, BLOCK, IDX_DIM), blk3),
                  pl.BlockSpec((BLOCK, IDX_HEADS), lambda b, i: (b * nb + i, 0)),
                  pl.BlockSpec((IDX_DIM, s_len), lambda b, i: (0, b)),
                  pl.BlockSpec((KV_WIDTH, s_len), lambda b, i: (0, b)),
                  pl.BlockSpec((N_KV_HEADS, s_len, LANES), lambda b, i: (0, b, 0))],
        out_specs=pl.BlockSpec((BLOCK, BRANCH), lambda b, i: (b * nb + i, 0)),
        out_shape=jax.ShapeDtypeStruct((m, BRANCH), F32),
        scratch_shapes=[pltpu.VMEM((BLOCK, s_len), F32), pltpu.VMEM((BLOCK, s_len), F32),
                        pltpu.VMEM((IDX_HEADS, BLOCK, LANES), F32),
                        pltpu.VMEM((N_KV_HEADS, 2, rows, KEY_CHUNK), F32),
                        pltpu.VMEM((N_KV_HEADS, 2, rows, KEY_CHUNK), BF16)],
        compiler_params=_params(2),
        name="dsa_prompt",
    )(qh, iqh, iw, ikt, kt, vx)


def _swa_prompt_kernel(sink_ref, qh_ref, ktp_ref, ktc_ref, vp_ref, vc_ref, o_ref):
    i = pl.program_id(1)
    rows = GROUP * BLOCK
    t = lax.broadcasted_iota(I32, (BLOCK, BLOCK), 0)
    j = lax.broadcasted_iota(I32, (BLOCK, BLOCK), 1)
    d_prev = BLOCK + t - j
    d_cur = t - j
    ok_prev = (d_prev >= 0) & (d_prev <= WINDOW) & (i > 0)
    ok_cur = (d_cur >= 0) & (d_cur <= WINDOW)
    b_prev = jnp.where(ok_prev, 0.0, NEG).astype(F32)
    b_cur = jnp.where(ok_cur, 0.0, NEG).astype(F32)
    for g in range(N_KV_HEADS):
        qg = qh_ref[g * GROUP:(g + 1) * GROUP].reshape(rows, HEAD_DIM)
        rs = slice(g * HEAD_DIM, (g + 1) * HEAD_DIM)
        sp = jnp.dot(qg, ktp_ref[rs, :], preferred_element_type=F32)
        sc = jnp.dot(qg, ktc_ref[rs, :], preferred_element_type=F32)
        sp = (sp.reshape(GROUP, BLOCK, BLOCK) + b_prev[None]).reshape(rows, BLOCK)
        sc = (sc.reshape(GROUP, BLOCK, BLOCK) + b_cur[None]).reshape(rows, BLOCK)
        sink = jnp.concatenate(
            [jnp.full((BLOCK, 1), sink_ref[g * GROUP + r] * LOG2E, F32) for r in range(GROUP)],
            axis=0)
        m = jnp.maximum(jnp.maximum(jnp.max(sp, axis=1, keepdims=True),
                                    jnp.max(sc, axis=1, keepdims=True)), sink)
        pp = jnp.exp2(sp - m).astype(BF16)
        pc = jnp.exp2(sc - m).astype(BF16)
        ox = (jnp.dot(pp, vp_ref[g], preferred_element_type=F32)
              + jnp.dot(pc, vc_ref[g], preferred_element_type=F32))
        denom = ox[:, HEAD_DIM:HEAD_DIM + 1] + jnp.exp2(sink - m)
        og = ox[:, :HEAD_DIM] * (1.0 / denom)
        for jj in range(GROUP // 2):
            pair = jnp.concatenate([og[(2 * jj) * BLOCK:(2 * jj + 1) * BLOCK],
                                    og[(2 * jj + 1) * BLOCK:(2 * jj + 2) * BLOCK]], axis=1)
            col = g * GROUP * HEAD_DIM + jj * LANES
            o_ref[:, col:col + LANES] = pair


def _swa_prompt(sinks_l, qh, kt, vx, batch, s_len):
    assert WINDOW <= BLOCK
    nb = s_len // BLOCK
    m = batch * s_len
    cur = lambda b, i: b * nb + i
    prev = lambda b, i: b * nb + jnp.maximum(i - 1, 0)
    return pl.pallas_call(
        _swa_prompt_kernel,
        grid=(batch, nb),
        in_specs=[pl.BlockSpec(memory_space=pltpu.SMEM),
                  pl.BlockSpec((N_HEADS, BLOCK, HEAD_DIM), lambda b, i: (0, cur(b, i), 0)),
                  pl.BlockSpec((KV_WIDTH, BLOCK), lambda b, i: (0, prev(b, i))),
                  pl.BlockSpec((KV_WIDTH, BLOCK), lambda b, i: (0, cur(b, i))),
                  pl.BlockSpec((N_KV_HEADS, BLOCK, LANES), lambda b, i: (0, prev(b, i), 0)),
                  pl.BlockSpec((N_KV_HEADS, BLOCK, LANES), lambda b, i: (0, cur(b, i), 0))],
        out_specs=pl.BlockSpec((BLOCK, BRANCH), lambda b, i: (cur(b, i), 0)),
        out_shape=jax.ShapeDtypeStruct((m, BRANCH), F32),
        compiler_params=_params(2),
        name="swa_prompt",
    )(sinks_l, qh, kt, kt, vx, vx)


def _block_diag_q(qh, n_seq, t_len):
    q = qh.reshape(N_KV_HEADS, GROUP, n_seq, t_len, HEAD_DIM)
    q = jnp.transpose(q, (2, 3, 0, 1, 4))
    eye = jnp.eye(N_KV_HEADS, dtype=q.dtype)
    qbd = q[:, :, :, :, None, :] * eye[None, None, :, None, :, None]
    return qbd.reshape(n_seq, t_len * N_HEADS, KV_WIDTH)


def _block_diag_out(acc, n_seq, t_len):
    a = acc.reshape(n_seq, t_len, N_KV_HEADS, GROUP, N_KV_HEADS, HEAD_DIM)
    idx = jnp.arange(N_KV_HEADS)
    a = a[:, :, idx, :, idx, :]
    return jnp.transpose(a, (1, 2, 0, 3, 4)).reshape(n_seq * t_len, BRANCH)


def _rows_from_tokens(x, t_len):
    return jnp.concatenate(
        [jnp.broadcast_to(x[t:t + 1], (N_HEADS, x.shape[1])) for t in range(t_len)], axis=0)


def _seqs_per_step(t_len):
    assert SUBLANES % t_len == 0
    return SUBLANES // t_len


def _page_specs(width, la, seqs):
    def spec(s, r):
        return pl.BlockSpec(
            (1, 1, width, PAGE_SIZE),
            lambda b, j, pt: (la, pt[b * seqs + s, j * PAGES_PER_STEP + r], 0, 0))
    return [spec(s, r) for s in range(seqs) for r in range(PAGES_PER_STEP)]


def _pages(refs, s):
    return jnp.concatenate([refs[s * PAGES_PER_STEP + r][0, 0] for r in range(PAGES_PER_STEP)],
                           axis=1).astype(BF16)


def _dsa_sample_score_kernel(pt_ref, iq_ref, w_ref, ikn_ref, *rest, t_len, seqs):
    ik_refs = rest[:seqs * PAGES_PER_STEP]
    sc_ref, scn_ref = rest[seqs * PAGES_PER_STEP:]
    j = pl.program_id(1)

    def scores(s, keys_t):
        x = jnp.dot(iq_ref[s], keys_t, preferred_element_type=F32)
        x = w_ref[s] * jnp.maximum(x, 0.0)
        return jnp.sum(x.reshape(t_len, IDX_HEADS, x.shape[1]), axis=1)

    sc_ref[...] = jnp.concatenate([scores(s, _pages(ik_refs, s)) for s in range(seqs)], axis=0)

    @pl.when(j == pl.num_programs(1) - 1)
    def _():
        tq = lax.broadcasted_iota(I32, (t_len, LANES), 0)
        kn = lax.broadcasted_iota(I32, (t_len, LANES), 1)
        scn_ref[...] = jnp.concatenate(
            [jnp.where(kn <= tq, scores(s, ikn_ref[s].astype(BF16)), -jnp.inf)
             for s in range(seqs)], axis=0)


def _dsa_sample_score(page_table, iq_rows, w_rows, ik_new_t, cache_ik_t, la, t_len):
    n_seq, n_pages = page_table.shape
    past = n_pages * PAGE_SIZE
    seqs = _seqs_per_step(t_len)
    assert n_pages % PAGES_PER_STEP == 0 and n_seq % seqs == 0
    n_steps = n_pages // PAGES_PER_STEP
    chunk = PAGES_PER_STEP * PAGE_SIZE
    seq3 = lambda b, j, pt: (b, 0, 0)
    grid_spec = pltpu.PrefetchScalarGridSpec(
        num_scalar_prefetch=1,
        grid=(n_seq // seqs, n_steps),
        in_specs=[pl.BlockSpec((seqs, t_len * IDX_HEADS, IDX_DIM), seq3),
                  pl.BlockSpec((seqs, t_len * IDX_HEADS, 1), seq3),
                  pl.BlockSpec((seqs, IDX_DIM, LANES), seq3)]
                 + _page_specs(IDX_DIM, la, seqs),
        out_specs=[pl.BlockSpec((SUBLANES, chunk), lambda b, j, pt: (b, j)),
                   pl.BlockSpec((SUBLANES, LANES), lambda b, j, pt: (b, 0))],
    )
    return pl.pallas_call(
        functools.partial(_dsa_sample_score_kernel, t_len=t_len, seqs=seqs),
        grid_spec=grid_spec,
        out_shape=[jax.ShapeDtypeStruct((n_seq * t_len, past), F32),
                   jax.ShapeDtypeStruct((n_seq * t_len, LANES), F32)],
        compiler_params=_params(2),
        name="dsa_sample_score",
    )(page_table, iq_rows, w_rows, ik_new_t, *([cache_ik_t] * (seqs * PAGES_PER_STEP)))


def _dsa_sample_select_kernel(sc_ref, scn_ref, bias_ref, biasn_ref, *, n_keep):
    past = sc_ref.shape[1]
    _select_bias([(sc_ref, bias_ref, past // KEY_CHUNK, KEY_CHUNK, 0),
                  (scn_ref, biasn_ref, 1, LANES, past)], n_keep, past + LANES)


def _dsa_sample_select(scores, scores_new, n_keep):
    n_rows, past = scores.shape
    rows = 128 if n_rows % 128 == 0 else 8
    assert n_rows % rows == 0 and past % KEY_CHUNK == 0
    row = lambda i: (i, 0)
    return pl.pallas_call(
        functools.partial(_dsa_sample_select_kernel, n_keep=n_keep),
        grid=(n_rows // rows,),
        in_specs=[pl.BlockSpec((rows, past), row), pl.BlockSpec((rows, LANES), row)],
        out_specs=[pl.BlockSpec((rows, past), row), pl.BlockSpec((rows, LANES), row)],
        out_shape=[jax.ShapeDtypeStruct((n_rows, past), F32),
                   jax.ShapeDtypeStruct((n_rows, LANES), F32)],
        compiler_params=_params(1),
        name="dsa_sample_select",
    )(scores, scores_new)


def _dsa_sample_attend_kernel(pt_ref, q_ref, bias_ref, biasn_ref, kn_ref, vn_ref, *rest,
                              t_len, seqs):
    n_pages = seqs * PAGES_PER_STEP
    k_refs = rest[:n_pages]
    v_refs = rest[n_pages:2 * n_pages]
    o_ref, m_ref, l_ref, acc_ref = rest[2 * n_pages:]
    j = pl.program_id(1)
    n_steps = pl.num_programs(1)

    @pl.when(j == 0)
    def _():
        m_ref[...] = jnp.full(m_ref.shape, NEG, F32)
        l_ref[...] = jnp.zeros(l_ref.shape, F32)
        acc_ref[...] = jnp.zeros(acc_ref.shape, F32)

    def update(s_i, k_t, v_t, bias_tok):
        q = q_ref[s_i]
        s = jnp.dot(q, k_t, preferred_element_type=F32) + _rows_from_tokens(bias_tok, t_len)
        m = m_ref[s_i]
        m_new = jnp.maximum(m, jnp.max(s, axis=1, keepdims=True))
        alpha = jnp.exp2(m - m_new)
        p = jnp.exp2(s - m_new)
        l_ref[s_i] = alpha * l_ref[s_i] + jnp.sum(p, axis=1, keepdims=True)
        acc_ref[s_i] = alpha * acc_ref[s_i] + lax.dot_general(
            p.astype(BF16), v_t, NT_DIMS, preferred_element_type=F32)
        m_ref[s_i] = m_new

    for s_i in range(seqs):
        tok = slice(s_i * t_len, (s_i + 1) * t_len)
        update(s_i, _pages(k_refs, s_i), _pages(v_refs, s_i), bias_ref[tok, :])

    @pl.when(j == n_steps - 1)
    def _():
        for s_i in range(seqs):
            tok = slice(s_i * t_len, (s_i + 1) * t_len)
            update(s_i, kn_ref[s_i].astype(BF16), vn_ref[s_i].astype(BF16), biasn_ref[tok, :])
            o_ref[s_i] = acc_ref[s_i] / l_ref[s_i]


def _dsa_sample_attend(page_table, q_bd, bias, bias_new, k_new_t, v_new_t, cache_k_t, cache_v_t,
                       la, t_len):
    n_seq, n_pages = page_table.shape
    n_steps = n_pages // PAGES_PER_STEP
    seqs = _seqs_per_step(t_len)
    rows = t_len * N_HEADS
    chunk = PAGES_PER_STEP * PAGE_SIZE
    seq3 = lambda b, j, pt: (b, 0, 0)
    grid_spec = pltpu.PrefetchScalarGridSpec(
        num_scalar_prefetch=1,
        grid=(n_seq // seqs, n_steps),
        in_specs=[pl.BlockSpec((seqs, rows, KV_WIDTH), seq3),
                  pl.BlockSpec((SUBLANES, chunk), lambda b, j, pt: (b, j)),
                  pl.BlockSpec((SUBLANES, LANES), lambda b, j, pt: (b, 0)),
                  pl.BlockSpec((seqs, KV_WIDTH, LANES), seq3),
                  pl.BlockSpec((seqs, KV_WIDTH, LANES), seq3)]
                 + _page_specs(KV_WIDTH, la, seqs) * 2,
        out_specs=pl.BlockSpec((seqs, rows, KV_WIDTH), seq3),
        scratch_shapes=[pltpu.VMEM((seqs, rows, 1), F32), pltpu.VMEM((seqs, rows, 1), F32),
                        pltpu.VMEM((seqs, rows, KV_WIDTH), F32)],
    )
    return pl.pallas_call(
        functools.partial(_dsa_sample_attend_kernel, t_len=t_len, seqs=seqs),
        grid_spec=grid_spec,
        out_shape=jax.ShapeDtypeStruct((n_seq, rows, KV_WIDTH), F32),
        compiler_params=_params(2),
        name="dsa_sample_attend",
    )(page_table, q_bd, bias, bias_new, k_new_t, v_new_t,
      *([cache_k_t] * (seqs * PAGES_PER_STEP)), *([cache_v_t] * (seqs * PAGES_PER_STEP)))


def _swa_sample_kernel(sink_ref, q_ref, kn_ref, vn_ref, wk_ref, wv_ref, o_ref, nwk_ref, nwv_ref,
                       *, t_len, seqs):
    wb = wk_ref.shape[2]
    rows = t_len * N_HEADS
    t_row = lax.broadcasted_iota(I32, (rows, wb), 0) // N_HEADS
    i_col = lax.broadcasted_iota(I32, (rows, wb), 1)
    d_state = wb + t_row - i_col
    b_state = jnp.where((d_state >= 0) & (d_state <= WINDOW), 0.0, NEG).astype(F32)
    t_row_n = lax.broadcasted_iota(I32, (rows, LANES), 0) // N_HEADS
    j_col = lax.broadcasted_iota(I32, (rows, LANES), 1)
    d_new = t_row_n - j_col
    b_new = jnp.where((d_new >= 0) & (d_new <= WINDOW) & (j_col < t_len), 0.0, NEG).astype(F32)
    h_row = lax.broadcasted_iota(I32, (rows, 1), 0) % N_HEADS
    sink = jnp.zeros((rows, 1), F32)
    for h in range(N_HEADS):
        sink = jnp.where(h_row == h, sink_ref[h] * LOG2E, sink)
    keep_old = lax.broadcasted_iota(I32, (KV_WIDTH, wb), 1) < wb - t_len
    for s_i in range(seqs):
        q = q_ref[s_i]
        wk = wk_ref[s_i]
        wv = wv_ref[s_i]
        kn = kn_ref[s_i]
        vn = vn_ref[s_i]
        ss = jnp.dot(q, wk.astype(BF16), preferred_element_type=F32) + b_state
        sn = jnp.dot(q, kn.astype(BF16), preferred_element_type=F32) + b_new
        m = jnp.maximum(jnp.maximum(jnp.max(ss, axis=1, keepdims=True),
                                    jnp.max(sn, axis=1, keepdims=True)), sink)
        ps = jnp.exp2(ss - m)
        pn = jnp.exp2(sn - m)
        denom = (jnp.sum(ps, axis=1, keepdims=True) + jnp.sum(pn, axis=1, keepdims=True)
                 + jnp.exp2(sink - m))
        inv = 1.0 / denom
        o_ref[s_i] = (lax.dot_general((ps * inv).astype(BF16), wv.astype(BF16), NT_DIMS,
                                      preferred_element_type=F32)
                      + lax.dot_general((pn * inv).astype(BF16), vn.astype(BF16), NT_DIMS,
                                        preferred_element_type=F32))
        nwk_ref[s_i] = jnp.where(keep_old, pltpu.roll(wk, wb - t_len, 1), pltpu.roll(kn, wb - t_len, 1))
        nwv_ref[s_i] = jnp.where(keep_old, pltpu.roll(wv, wb - t_len, 1), pltpu.roll(vn, wb - t_len, 1))


def _swa_sample(sinks_l, q_bd, k_new_t, v_new_t, state_wk_t, state_wv_t, lb, t_len):
    _, n_seq, _, wb = state_wk_t.shape
    seqs = 8 if n_seq % 8 == 0 else 1
    rows = t_len * N_HEADS
    assert wb == LANES and wb >= t_len
    seq3 = lambda b: (b, 0, 0)
    st4 = lambda b: (lb, b, 0, 0)
    return pl.pallas_call(
        functools.partial(_swa_sample_kernel, t_len=t_len, seqs=seqs),
        grid=(n_seq // seqs,),
        in_specs=[pl.BlockSpec(memory_space=pltpu.SMEM),
                  pl.BlockSpec((seqs, rows, KV_WIDTH), seq3),
                  pl.BlockSpec((seqs, KV_WIDTH, LANES), seq3),
                  pl.BlockSpec((seqs, KV_WIDTH, LANES), seq3),
                  pl.BlockSpec((None, seqs, KV_WIDTH, wb), st4),
                  pl.BlockSpec((None, seqs, KV_WIDTH, wb), st4)],
        out_specs=[pl.BlockSpec((seqs, rows, KV_WIDTH), seq3),
                   pl.BlockSpec((seqs, KV_WIDTH, wb), seq3),
                   pl.BlockSpec((seqs, KV_WIDTH, wb), seq3)],
        out_shape=[jax.ShapeDtypeStruct((n_seq, rows, KV_WIDTH), F32),
                   jax.ShapeDtypeStruct((n_seq, KV_WIDTH, wb), F32),
                   jax.ShapeDtypeStruct((n_seq, KV_WIDTH, wb), F32)],
        compiler_params=_params(1),
        name="swa_sample",
    )(sinks_l, q_bd, k_new_t, v_new_t, state_wk_t, state_wv_t)


def _new_rows_t(x, n_seq, t_len):
    xt = jnp.transpose(x.reshape(n_seq, t_len, x.shape[1]), (0, 2, 1))
    return jnp.pad(xt, ((0, 0), (0, 0), (0, LANES - t_len)))


def kernel(x_prompt, x_sample, cache_k, cache_v, cache_ik, page_table, state_wk, state_wv,
           norm_g, final_g, w_in_a, w_out_a, w_in_b, w_out_b, sinks):
    batch, s_len, d = x_prompt.shape
    n_seq, t_len, _ = x_sample.shape
    depth = norm_g.shape[0]
    n_pool = cache_k.shape[1]
    past = page_table.shape[1] * PAGE_SIZE
    wb = state_wk.shape[2]
    wb_p = min(WINDOW, s_len)

    cos_p, sin_p = _rope_tables(jnp.tile(jnp.arange(s_len, dtype=I32), batch))
    cos_s, sin_s = _rope_tables(jnp.tile(past + jnp.arange(t_len, dtype=I32), n_seq))
    ck_t = jnp.transpose(cache_k, (0, 1, 3, 4, 2)).reshape(cache_k.shape[0], n_pool, KV_WIDTH, PAGE_SIZE)
    cv_t = jnp.transpose(cache_v, (0, 1, 3, 4, 2)).reshape(cache_v.shape[0], n_pool, KV_WIDTH, PAGE_SIZE)
    cik_t = jnp.transpose(cache_ik, (0, 1, 3, 2))
    swk_t = jnp.transpose(state_wk, (0, 1, 3, 4, 2)).reshape(state_wk.shape[0], n_seq, KV_WIDTH, wb)
    swv_t = jnp.transpose(state_wv, (0, 1, 3, 4, 2)).reshape(state_wv.shape[0], n_seq, KV_WIDTH, wb)
    fg = final_g.reshape(1, d)

    hp = x_prompt.reshape(batch * s_len, d)
    hs = x_sample.reshape(n_seq * t_len, d)
    kp_l, vp_l, ikp_l, ks_l, vs_l, iks_l = [], [], [], [], [], []
    wkp_l, wvp_l, wks_l, wvs_l = [], [], [], []
    for i in range(depth):
        g = norm_g[i].reshape(1, d)
        final = i == depth - 1
        if i % N_MIXERS == 0:
            la = i // N_MIXERS
            w_in = jnp.pad(w_in_a[la], ((0, 0), (0, A_PAD_COLS - w_in_a.shape[2]))).astype(BF16)
            w_out = w_out_a[la].astype(BF16)
            qh, k32, v32, kt, vh, gate, iqh, ik32, ikt, iw = _project(hp, g, w_in, cos_p, sin_p, "a")
            o = _dsa_prompt(qh, iqh, iw, ikt, kt, vh, batch, s_len)
            hp = _gated_out(o, gate, w_out, hp, fg, final)
            kp_l.append(k32.reshape(batch, s_len, N_KV_HEADS, HEAD_DIM))
            vp_l.append(v32.reshape(batch, s_len, N_KV_HEADS, HEAD_DIM))
            ikp_l.append(ik32.reshape(batch, s_len, IDX_DIM))
            qh, k32, v32, kt, vh, gate, iqh, ik32, ikt, iw = _project(hs, g, w_in, cos_s, sin_s, "a")
            iq_rows = jnp.transpose(iqh.reshape(IDX_HEADS, n_seq, t_len, IDX_DIM),
                                    (1, 2, 0, 3)).reshape(n_seq, t_len * IDX_HEADS, IDX_DIM)
            w_rows = iw.reshape(n_seq, t_len * IDX_HEADS, 1)
            sc, sc_new = _dsa_sample_score(page_table, iq_rows, w_rows,
                                           _new_rows_t(ik32, n_seq, t_len), cik_t, la, t_len)
            bias, bias_new = _dsa_sample_select(sc, sc_new, min(TOPK_MAX, (past + t_len) // 4))
            acc = _dsa_sample_attend(page_table, _block_diag_q(qh, n_seq, t_len), bias, bias_new,
                                     _new_rows_t(k32, n_seq, t_len), _new_rows_t(v32, n_seq, t_len),
                                     ck_t, cv_t, la, t_len)
            hs = _gated_out(_block_diag_out(acc, n_seq, t_len), gate, w_out, hs, fg, final)
            ks_l.append(k32.reshape(n_seq, t_len, N_KV_HEADS, HEAD_DIM))
            vs_l.append(v32.reshape(n_seq, t_len, N_KV_HEADS, HEAD_DIM))
            iks_l.append(ik32.reshape(n_seq, t_len, IDX_DIM))
        else:
            lb = i // N_MIXERS
            w_in = w_in_b[lb].astype(BF16)
            w_out = w_out_b[lb].astype(BF16)
            qh, k32, v32, kt, vh, gate = _project(hp, g, w_in, cos_p, sin_p, "b")
            o = _swa_prompt(sinks[lb], qh, kt, vh, batch, s_len)
            hp = _gated_out(o, gate, w_out, hp, fg, final)
            k4 = k32.reshape(batch, s_len, N_KV_HEADS, HEAD_DIM)
            v4 = v32.reshape(batch, s_len, N_KV_HEADS, HEAD_DIM)
            wkp_l.append(k4[:, s_len - wb_p:])
            wvp_l.append(v4[:, s_len - wb_p:])
            qh, k32, v32, kt, vh, gate = _project(hs, g, w_in, cos_s, sin_s, "b")
            acc, nwk, nwv = _swa_sample(sinks[lb], _block_diag_q(qh, n_seq, t_len),
                                        _new_rows_t(k32, n_seq, t_len), _new_rows_t(v32, n_seq, t_len),
                                        swk_t, swv_t, lb, t_len)
            hs = _gated_out(_block_diag_out(acc, n_seq, t_len), gate, w_out, hs, fg, final)
            back = lambda a: jnp.transpose(a.reshape(n_seq, N_KV_HEADS, HEAD_DIM, wb), (0, 3, 1, 2))
            wks_l.append(back(nwk))
            wvs_l.append(back(nwv))

    assert depth >= 1
    return (hp.reshape(batch, s_len, d), hs.reshape(n_seq, t_len, d),
            jnp.stack(kp_l), jnp.stack(vp_l), jnp.stack(ikp_l),
            jnp.stack(ks_l), jnp.stack(vs_l), jnp.stack(iks_l),
            jnp.stack(wkp_l), jnp.stack(wvp_l), jnp.stack(wks_l), jnp.stack(wvs_l))
```
